```python
import jax, jax.numpy as jnp
from jax import lax
import numpy as np

D_MODEL = 1024
BATCH = 8
SEQ = 4096
DEPTH = 4

N_EVEN = (DEPTH + 1) // 2
N_ODD = DEPTH // 2

NSA_HEADS = 8
HEAD_DIM = 64
KV_GROUPS = 2
HEADS_PER_GROUP = NSA_HEADS // KV_GROUPS
N_BRANCH = 3
CMP_BLOCK = 32
CMP_STRIDE = 16
CMP_HIDDEN = 128
SEL_BLOCK = 64
SEL_TOPK = 8
WINDOW = 512
Q_BLOCK = 128
FORCE_BONUS = 1e4
NEG = -1e30

POOL_WIDTH = D_MODEL // 2
POOL_WINDOWS = (2, 4, 8, 16)
POOL_GROUPS = len(POOL_WINDOWS)
POOL_GROUP_DIM = POOL_WIDTH // POOL_GROUPS

NSA_WIDTH = NSA_HEADS * HEAD_DIM
KV_WIDTH = KV_GROUPS * HEAD_DIM
GATE_WIDTH = NSA_HEADS * N_BRANCH
IN_WIDTH = NSA_WIDTH + N_BRANCH * 2 * KV_WIDTH + GATE_WIDTH + POOL_WIDTH
MIX_WIDTH = NSA_WIDTH + POOL_WIDTH

CONV_WIDTH = 31

FFN_HIDDEN = -(-8 * D_MODEL // (3 * 256)) * 256

kernel_name = "hybrid_nsa_pool_conformer_swiglu"


def rmsnorm(x, g, eps=1e-6):
    xf = x.astype(jnp.float32)
    y = xf * lax.rsqrt(jnp.mean(xf * xf, axis=-1, keepdims=True) + eps)
    return (y * g.astype(jnp.float32)).astype(x.dtype)


def layernorm(x, g, b, eps=1e-5):
    xf = x.astype(jnp.float32)
    mu = jnp.mean(xf, axis=-1, keepdims=True)
    xc = xf - mu
    var = jnp.mean(xc * xc, axis=-1, keepdims=True)
    y = xc * lax.rsqrt(var + eps) * g.astype(jnp.float32) + b.astype(jnp.float32)
    return y.astype(x.dtype)


def alibi_slopes():
    h = jnp.arange(1, NSA_HEADS + 1, dtype=jnp.float32)
    return jnp.exp2(-8.0 * h / NSA_HEADS).reshape(KV_GROUPS, HEADS_PER_GROUP)


def compress_blocks(kv, pos, w1, w2):
    b, g, s, dh = kv.shape
    r = CMP_BLOCK // CMP_STRIDE
    n_cmp = s // CMP_STRIDE - r + 1
    sub = kv.reshape(b, g, s // CMP_STRIDE, CMP_STRIDE, dh)
    blocks = jnp.concatenate([sub[:, :, j:j + n_cmp] for j in range(r)], axis=3)
    blocks = (blocks + pos).reshape(b, g, n_cmp, CMP_BLOCK * dh)
    return jax.nn.gelu(blocks @ w1) @ w2


def compressed_branch(q, kc, vc, slopes):
    s = q.shape[3]
    n_cmp = kc.shape[2]
    t = jnp.arange(s, dtype=jnp.int32)[:, None]
    end = jnp.arange(n_cmp, dtype=jnp.int32)[None, :] * CMP_STRIDE + (CMP_BLOCK - 1)
    dist = (t - end).astype(jnp.float32)
    valid = dist >= 0
    logits = jnp.einsum('bgrtd,bgnd->bgrtn', q, kc).astype(jnp.float32)
    logits = logits - slopes[:, :, None, None] * dist
    logits = jnp.where(valid, logits, NEG)
    p = jax.nn.softmax(logits, axis=-1) * valid
    o = jnp.einsum('bgrtn,bgnd->bgrtd', p.astype(vc.dtype), vc)
    return o, p


def selection_indices(p_cmp, s):
    n_cmp = p_cmp.shape[-1]
    n_sel = s // SEL_BLOCK
    c0 = np.arange(n_cmp)[:, None] * CMP_STRIDE
    s0 = np.arange(n_sel)[None, :] * SEL_BLOCK
    overlap = np.clip(np.minimum(c0 + CMP_BLOCK, s0 + SEL_BLOCK) - np.maximum(c0, s0), 0, None) / CMP_BLOCK
    imp = jnp.einsum('bgtn,nj->bgtj', p_cmp.sum(axis=2), jnp.asarray(overlap, jnp.float32))
    t_blk = (jnp.arange(s, dtype=jnp.int32) // SEL_BLOCK)[:, None]
    j = jnp.arange(n_sel, dtype=jnp.int32)[None, :]
    forced = (j == 0) | (j == t_blk) | (j == t_blk - 1)
    imp = jnp.where(forced, imp + FORCE_BONUS, imp)
    imp = jnp.where(j > t_blk, NEG, imp)
    _, idx = lax.top_k(imp, min(SEL_TOPK, n_sel))
    return idx


def selected_branch(q, ks, vs, idx, slopes):
    b, g, r, s, dh = q.shape
    k = idx.shape[-1]
    n_sel = s // SEL_BLOCK
    nq = s // Q_BLOCK
    kb = ks.reshape(b, g, n_sel, SEL_BLOCK, dh)
    vb = vs.reshape(b, g, n_sel, SEL_BLOCK, dh)
    qc = jnp.moveaxis(q.reshape(b, g, r, nq, Q_BLOCK, dh), 3, 0)
    ic = jnp.moveaxis(idx.reshape(b, g, nq, Q_BLOCK, k), 2, 0)
    tc = jnp.arange(s, dtype=jnp.int32).reshape(nq, Q_BLOCK)
    bi = jnp.arange(b)[:, None, None, None]
    gi = jnp.arange(g)[None, :, None, None]
    offs = jnp.arange(SEL_BLOCK, dtype=jnp.int32)

    def one_block(args):
        qq, ii, tt = args
        k_sel = kb[bi, gi, ii].reshape(b, g, Q_BLOCK, k * SEL_BLOCK, dh)
        v_sel = vb[bi, gi, ii].reshape(b, g, Q_BLOCK, k * SEL_BLOCK, dh)
        pos = (ii[..., None] * SEL_BLOCK + offs).reshape(b, g, Q_BLOCK, k * SEL_BLOCK)
        dist = (tt[None, None, :, None] - pos).astype(jnp.float32)[:, :, None]
        logits = jnp.einsum('bgrqd,bgqkd->bgrqk', qq, k_sel).astype(jnp.float32)
        logits = logits - slopes[None, :, :, None, None] * dist
        logits = jnp.where(dist >= 0, logits, NEG)
        p = jax.nn.softmax(logits, axis=-1)
        return jnp.einsum('bgrqk,bgqkd->bgrqd', p.astype(v_sel.dtype), v_sel)

    o = lax.map(one_block, (qc, ic, tc))
    return jnp.moveaxis(o, 0, 3).reshape(b, g, r, s, dh)


def window_branch(q, kw, vw, slopes):
    b, g, r, s, dh = q.shape
    nq = s // Q_BLOCK
    nw = WINDOW // Q_BLOCK
    pad = ((0, 0), (0, 0), (WINDOW, 0), (0, 0))
    kp = jnp.pad(kw, pad).reshape(b, g, nq + nw, Q_BLOCK, dh)
    vp = jnp.pad(vw, pad).reshape(b, g, nq + nw, Q_BLOCK, dh)
    k_band = jnp.concatenate([kp[:, :, j:j + nq] for j in range(nw + 1)], axis=3)
    v_band = jnp.concatenate([vp[:, :, j:j + nq] for j in range(nw + 1)], axis=3)
    qb = q.reshape(b, g, r, nq, Q_BLOCK, dh)
    t = jnp.arange(s, dtype=jnp.int32).reshape(nq, Q_BLOCK)
    kpos = (jnp.arange(nq, dtype=jnp.int32)[:, None] - nw) * Q_BLOCK + jnp.arange((nw + 1) * Q_BLOCK, dtype=jnp.int32)[None, :]
    dist = t[:, :, None] - kpos[:, None, :]
    valid = (dist >= 0) & (dist < WINDOW) & (kpos[:, None, :] >= 0)
    logits = jnp.einsum('bgrnqd,bgnkd->bgrnqk', qb, k_band).astype(jnp.float32)
    logits = logits - slopes[:, :, None, None, None] * dist.astype(jnp.float32)
    logits = jnp.where(valid, logits, NEG)
    p = jax.nn.softmax(logits, axis=-1)
    o = jnp.einsum('bgrnqk,bgnkd->bgrnqd', p.astype(v_band.dtype), v_band)
    return o.reshape(b, g, r, s, dh)


def multiscale_pool(u, w_pool, scale):
    b, s, c = u.shape
    uf = u.astype(jnp.float32)
    cs = jnp.pad(jnp.cumsum(uf, axis=1), ((0, 0), (1, 0), (0, 0)))
    t = jnp.arange(s, dtype=jnp.float32)
    outs = []
    for gi, w in enumerate(POOL_WINDOWS):
        sl = slice(gi * POOL_GROUP_DIM, (gi + 1) * POOL_GROUP_DIM)
        csg = cs[:, :, sl]
        start = jnp.pad(csg, ((0, 0), (w - 1, 0), (0, 0)))[:, :s]
        mean = (csg[:, 1:] - start) / jnp.minimum(t + 1.0, float(w))[None, :, None]
        outs.append(mean - uf[:, :, sl])
    d = jnp.stack(outs, axis=2).astype(u.dtype)
    y = jnp.einsum('bsgc,gce->bsge', d, w_pool).reshape(b, s, c)
    return y * scale


def nsa_pool_mixer(h, w_in, pos_k, pos_v, k_w1, k_w2, v_w1, v_w2, w_pool, pool_scale, w_out):
    b, s, _ = h.shape
    proj = h @ w_in
    sizes = [NSA_WIDTH] + [KV_WIDTH] * (2 * N_BRANCH) + [GATE_WIDTH]
    cuts = [int(c) for c in np.cumsum(sizes)]
    q, kc, vc, ks, vs, kw, vw, gate, u = jnp.split(proj, cuts, axis=-1)

    def kv_heads(z):
        return z.reshape(b, s, KV_GROUPS, HEAD_DIM).transpose(0, 2, 1, 3)

    q = q.reshape(b, s, KV_GROUPS, HEADS_PER_GROUP, HEAD_DIM).transpose(0, 2, 3, 1, 4) * (HEAD_DIM ** -0.5)
    slopes = alibi_slopes()

    kcmp = compress_blocks(kv_heads(kc), pos_k, k_w1, k_w2)
    vcmp = compress_blocks(kv_heads(vc), pos_v, v_w1, v_w2)
    o_cmp, p_cmp = compressed_branch(q, kcmp, vcmp, slopes)
    idx = selection_indices(p_cmp, s)
    o_sel = selected_branch(q, kv_heads(ks), kv_heads(vs), idx, slopes)
    o_win = window_branch(q, kv_heads(kw), kv_heads(vw), slopes)

    g = jax.nn.sigmoid(gate.reshape(b, s, KV_GROUPS, HEADS_PER_GROUP, N_BRANCH)).transpose(0, 2, 3, 1, 4)
    o = g[..., 0:1] * o_cmp + g[..., 1:2] * o_sel + g[..., 2:3] * o_win
    o = o.transpose(0, 3, 1, 2, 4).reshape(b, s, NSA_WIDTH)
    y_pool = multiscale_pool(u, w_pool, pool_scale)
    return jnp.concatenate([o, y_pool], axis=-1) @ w_out


def conformer_conv(h, w_pw1, b_pw1, w_dw, b_dw, ln_g, ln_b, w_pw2, b_pw2):
    a, gt = jnp.split(h @ w_pw1 + b_pw1, 2, axis=-1)
    z = a * jax.nn.sigmoid(gt)
    z = lax.conv_general_dilated(
        z, w_dw[:, None, :], window_strides=(1,), padding=[(CONV_WIDTH - 1, 0)],
        dimension_numbers=('NWC', 'WIO', 'NWC'), feature_group_count=D_MODEL) + b_dw
    z = jax.nn.silu(layernorm(z, ln_g, ln_b))
    return z @ w_pw2 + b_pw2


def swiglu(h, w_gate, w_up, w_down):
    return (jax.nn.silu(h @ w_gate) * (h @ w_up)) @ w_down


def setup_inputs(seed: int = 0) -> dict:
    key = jax.random.key(seed)
    keys = iter(jax.random.split(key, 32))

    def nrm(shape, scale):
        return jax.random.normal(next(keys), shape, jnp.float32) * scale

    def gain(shape):
        return 1.0 + nrm(shape, 0.05)

    dh = HEAD_DIM
    return {
        "x": nrm((BATCH, SEQ, D_MODEL), 1.0),
        "mix_norm": gain((DEPTH, D_MODEL)),
        "ffn_norm": gain((DEPTH, D_MODEL)),
        "nsa_w_in": nrm((N_EVEN, D_MODEL, IN_WIDTH), D_MODEL ** -0.5),
        "cmp_pos_k": nrm((N_EVEN, CMP_BLOCK, dh), 0.1),
        "cmp_pos_v": nrm((N_EVEN, CMP_BLOCK, dh), 0.1),
        "cmp_k_w1": nrm((N_EVEN, CMP_BLOCK * dh, CMP_HIDDEN), (CMP_BLOCK * dh) ** -0.5),
        "cmp_k_w2": nrm((N_EVEN, CMP_HIDDEN, dh), CMP_HIDDEN ** -0.5),
        "cmp_v_w1": nrm((N_EVEN, CMP_BLOCK * dh, CMP_HIDDEN), (CMP_BLOCK * dh) ** -0.5),
        "cmp_v_w2": nrm((N_EVEN, CMP_HIDDEN, dh), CMP_HIDDEN ** -0.5),
        "pool_w": nrm((N_EVEN, POOL_GROUPS, POOL_GROUP_DIM, POOL_GROUP_DIM), POOL_GROUP_DIM ** -0.5),
        "pool_scale": gain((N_EVEN, POOL_WIDTH)),
        "mix_w_out": nrm((N_EVEN, MIX_WIDTH, D_MODEL), MIX_WIDTH ** -0.5),
        "conv_w_pw1": nrm((N_ODD, D_MODEL, 2 * D_MODEL), D_MODEL ** -0.5),
        "conv_b_pw1": nrm((N_ODD, 2 * D_MODEL), 0.02),
        "conv_w_dw": nrm((N_ODD, CONV_WIDTH, D_MODEL), CONV_WIDTH ** -0.5),
        "conv_b_dw": nrm((N_ODD, D_MODEL), 0.02),
        "conv_ln_g": gain((N_ODD, D_MODEL)),
        "conv_ln_b": nrm((N_ODD, D_MODEL), 0.02),
        "conv_w_pw2": nrm((N_ODD, D_MODEL, D_MODEL), D_MODEL ** -0.5),
        "conv_b_pw2": nrm((N_ODD, D_MODEL), 0.02),
        "ffn_w_gate": nrm((DEPTH, D_MODEL, FFN_HIDDEN), D_MODEL ** -0.5),
        "ffn_w_up": nrm((DEPTH, D_MODEL, FFN_HIDDEN), D_MODEL ** -0.5),
        "ffn_w_down": nrm((DEPTH, FFN_HIDDEN, D_MODEL), FFN_HIDDEN ** -0.5),
        "final_norm": gain((D_MODEL,)),
    }


def reference(x, mix_norm, ffn_norm, nsa_w_in, cmp_pos_k, cmp_pos_v, cmp_k_w1, cmp_k_w2,
              cmp_v_w1, cmp_v_w2, pool_w, pool_scale, mix_w_out, conv_w_pw1, conv_b_pw1,
              conv_w_dw, conv_b_dw, conv_ln_g, conv_ln_b, conv_w_pw2, conv_b_pw2,
              ffn_w_gate, ffn_w_up, ffn_w_down, final_norm):
    for layer in range(DEPTH):
        h = rmsnorm(x, mix_norm[layer])
        i = layer // 2
        if layer % 2 == 0:
            x = x + nsa_pool_mixer(h, nsa_w_in[i], cmp_pos_k[i], cmp_pos_v[i], cmp_k_w1[i], cmp_k_w2[i],
                                   cmp_v_w1[i], cmp_v_w2[i], pool_w[i], pool_scale[i], mix_w_out[i])
        else:
            x = x + conformer_conv(h, conv_w_pw1[i], conv_b_pw1[i], conv_w_dw[i], conv_b_dw[i],
                                   conv_ln_g[i], conv_ln_b[i], conv_w_pw2[i], conv_b_pw2[i])
        h = rmsnorm(x, ffn_norm[layer])
        x = x + swiglu(h, ffn_w_gate[layer], ffn_w_up[layer], ffn_w_down[layer])
    return rmsnorm(x, final_norm)
```

```python
import functools

import numpy as np
import jax
import jax.numpy as jnp
from jax import lax
from jax.experimental import pallas as pl
from jax.experimental.pallas import tpu as pltpu

F32 = jnp.float32
BF16 = jnp.bfloat16

NSA_HEADS = 8
HEAD_DIM = 64
KV_GROUPS = 2
HEADS_PER_GROUP = NSA_HEADS // KV_GROUPS
N_BRANCH = 3
CMP_BLOCK = 32
CMP_STRIDE = 16
CMP_HIDDEN = 128
SEL_BLOCK = 64
SEL_TOPK = 8
WINDOW = 512
FORCE_BONUS = 1e4
NEG = -1e30
POOL_WINDOWS = (2, 4, 8, 16)
POOL_GROUP_DIM = 128
POOL_WIDTH = POOL_GROUP_DIM * len(POOL_WINDOWS)
NSA_WIDTH = NSA_HEADS * HEAD_DIM
KV_WIDTH = KV_GROUPS * HEAD_DIM
GATE_WIDTH = NSA_HEADS * N_BRANCH
GATE_ROWS = 32
CONV_WIDTH = 31
CONV_HALO = 32
POOL_HALO = 16

V7X_VMEM_BYTES = 64 * 1024 * 1024
LANES = 128

Q_TILE = 128
KEY_TILE = 128
PROJ_ROWS = 512
CONV_ROWS = 256
FFN_CHUNK = 256

NT_DIMS = (((1,), (1,)), ((), ()))
TN_DIMS = (((0,), (0,)), ((), ()))


def _params(semantics, vmem_bytes):
    return pltpu.CompilerParams(dimension_semantics=semantics,
                                vmem_limit_bytes=min(int(vmem_bytes), V7X_VMEM_BYTES - (8 << 20)))


def _const_spec(shape):
    nd = len(shape)
    return pl.BlockSpec(shape, lambda *_: (0,) * nd, pipeline_mode=pl.Buffered(1))


def _dot(a, b):
    return jnp.dot(a, b, preferred_element_type=F32)


def _rms(x, g, eps=1e-6):
    ms = jnp.mean(x * x, axis=-1, keepdims=True)
    return x * lax.rsqrt(ms + eps) * g


def _inproj_kernel(x_ref, g_ref, wn_ref, wt_ref, cmp_ref, kn_ref, u_ref, qt_ref, vt_ref, gt_ref):
    h = _rms(x_ref[0], g_ref[...]).astype(BF16)
    pn = _dot(h, wn_ref[...])
    pt = lax.dot_general(wt_ref[...], h, NT_DIMS, preferred_element_type=F32)
    for c in range(2 * KV_GROUPS):
        cmp_ref[c, 0] = pn[:, c * HEAD_DIM:(c + 1) * HEAD_DIM]
        kn_ref[c, 0] = pn[:, 2 * KV_WIDTH + c * HEAD_DIM:2 * KV_WIDTH + (c + 1) * HEAD_DIM].astype(BF16)
    u_ref[0] = pn[:, 4 * KV_WIDTH:]
    qt_ref[0] = (pt[:NSA_WIDTH] * (HEAD_DIM ** -0.5)).astype(BF16)
    vt_ref[0] = pt[NSA_WIDTH:NSA_WIDTH + 2 * KV_WIDTH].astype(BF16)
    gt_ref[0] = jax.nn.sigmoid(pt[NSA_WIDTH + 2 * KV_WIDTH:])


def _inproj(x, g, wn, wt):
    b, s, d = x.shape
    tm = min(PROJ_ROWS, s)
    n_norm = wn.shape[1]
    n_tr = wt.shape[0]
    grid = (b, s // tm)
    out_shape = (
        jax.ShapeDtypeStruct((2 * KV_GROUPS, b, s, HEAD_DIM), F32),
        jax.ShapeDtypeStruct((2 * KV_GROUPS, b, s, HEAD_DIM), BF16),
        jax.ShapeDtypeStruct((b, s, POOL_WIDTH), F32),
        jax.ShapeDtypeStruct((b, NSA_WIDTH, s), BF16),
        jax.ShapeDtypeStruct((b, 2 * KV_WIDTH, s), BF16),
        jax.ShapeDtypeStruct((b, GATE_ROWS, s), F32),
    )
    return pl.pallas_call(
        _inproj_kernel,
        grid=grid,
        in_specs=[
            pl.BlockSpec((1, tm, d), lambda bi, i: (bi, i, 0)),
            _const_spec((1, d)),
            _const_spec((d, n_norm)),
            _const_spec((n_tr, d)),
        ],
        out_specs=(
            pl.BlockSpec((2 * KV_GROUPS, 1, tm, HEAD_DIM), lambda bi, i: (0, bi, i, 0)),
            pl.BlockSpec((2 * KV_GROUPS, 1, tm, HEAD_DIM), lambda bi, i: (0, bi, i, 0)),
            pl.BlockSpec((1, tm, POOL_WIDTH), lambda bi, i: (bi, i, 0)),
            pl.BlockSpec((1, NSA_WIDTH, tm), lambda bi, i: (bi, 0, i)),
            pl.BlockSpec((1, 2 * KV_WIDTH, tm), lambda bi, i: (bi, 0, i)),
            pl.BlockSpec((1, GATE_ROWS, tm), lambda bi, i: (bi, 0, i)),
        ),
        out_shape=out_shape,
        compiler_params=_params(("parallel", "parallel"), 40 << 20),
        name="nsa_inproj",
    )(x, g, wn, wt)


def _compress_kernel(sub_ref, pos_ref, w1_ref, w2_ref, w2t_ref, cn_ref, ct_ref):
    sub = sub_ref[0, 0]
    a = _dot((sub + pos_ref[0, 0]).astype(BF16), w1_ref[0, 0])
    bb = _dot((sub + pos_ref[0, 1]).astype(BF16), w1_ref[0, 1])
    nrow = sub.shape[0]
    hid = jax.nn.gelu(a + pltpu.roll(bb, nrow - 1, 0), approximate=True).astype(BF16)
    cn_ref[0, 0] = _dot(hid, w2_ref[0]).astype(BF16)
    ct_ref[0, 0] = lax.dot_general(w2t_ref[0], hid, NT_DIMS, preferred_element_type=F32).astype(BF16)


def _compress(cmp_in, pos, w1, w2, w2t):
    nc4, b, s, dh = cmp_in.shape
    nsub = s // CMP_STRIDE
    sub = cmp_in.reshape(nc4, b, nsub, CMP_STRIDE * dh)
    half = CMP_STRIDE * dh
    return pl.pallas_call(
        _compress_kernel,
        grid=(nc4, b),
        in_specs=[
            pl.BlockSpec((1, 1, nsub, half), lambda c, bi: (c, bi, 0, 0)),
            pl.BlockSpec((1, 2, 1, half), lambda c, bi: (c // KV_GROUPS, 0, 0, 0)),
            pl.BlockSpec((1, 2, half, CMP_HIDDEN), lambda c, bi: (c // KV_GROUPS, 0, 0, 0)),
            pl.BlockSpec((1, CMP_HIDDEN, dh), lambda c, bi: (c // KV_GROUPS, 0, 0)),
            pl.BlockSpec((1, dh, CMP_HIDDEN), lambda c, bi: (c // KV_GROUPS, 0, 0)),
        ],
        out_specs=(
            pl.BlockSpec((1, 1, nsub, dh), lambda c, bi: (c, bi, 0, 0)),
            pl.BlockSpec((1, 1, dh, nsub), lambda c, bi: (c, bi, 0, 0)),
        ),
        out_shape=(
            jax.ShapeDtypeStruct((nc4, b, nsub, dh), BF16),
            jax.ShapeDtypeStruct((nc4, b, dh, nsub), BF16),
        ),
        compiler_params=_params(("parallel", "parallel"), 24 << 20),
        name="nsa_compress",
    )(sub, pos, w1, w2, w2t)


def _heads_on_lanes(qt):
    return jnp.concatenate([qt[r * HEAD_DIM:(r + 1) * HEAD_DIM] for r in range(HEADS_PER_GROUP)], axis=1)


def _heads_on_rows(o, tq):
    return jnp.concatenate([o[:, r * tq:(r + 1) * tq] for r in range(HEADS_PER_GROUP)], axis=0)


def _cmp_kernel(q_ref, kc_ref, vct_ref, slope_ref, ov_ref, o_ref, sel_ref, *, tq):
    i = pl.program_id(2)
    lanes = HEADS_PER_GROUP * tq
    q = _heads_on_lanes(q_ref[0])
    s = _dot(kc_ref[0, 0], q)
    nc = s.shape[0]
    blk_end = lax.broadcasted_iota(jnp.int32, (nc, lanes), 0) * CMP_STRIDE + (CMP_BLOCK - 1)
    t = i * tq + (lax.broadcasted_iota(jnp.int32, (nc, lanes), 1) & (tq - 1))
    dist = (t - blk_end).astype(F32)
    valid = dist >= 0
    logits = jnp.where(valid, s - slope_ref[0] * dist, NEG)
    m = jnp.max(logits, axis=0, keepdims=True)
    e = jnp.where(valid, jnp.exp(logits - m), 0.0)
    l = jnp.sum(e, axis=0, keepdims=True)
    p = e / jnp.where(l > 0, l, 1.0)
    o = _dot(vct_ref[0, 0], p.astype(BF16))
    o_ref[0] = _heads_on_rows(o, tq)

    psum = p[:, 0:tq]
    for r in range(1, HEADS_PER_GROUP):
        psum = psum + p[:, r * tq:(r + 1) * tq]
    hi = psum.astype(BF16)
    rem = psum - hi.astype(F32)
    mid = rem.astype(BF16)
    lo = (rem - mid.astype(F32)).astype(BF16)
    ov = ov_ref[...]
    imp = _dot(ov, hi) + _dot(ov, mid) + _dot(ov, lo)

    n_sel = imp.shape[0]
    j = lax.broadcasted_iota(jnp.int32, (n_sel, tq), 0)
    t_blk = (i * tq + lax.broadcasted_iota(jnp.int32, (n_sel, tq), 1)) // SEL_BLOCK
    forced = (j == 0) | (j == t_blk) | (j == t_blk - 1)
    imp = jnp.where(forced, imp + FORCE_BONUS, imp)
    imp = jnp.where(j > t_blk, NEG, imp)
    chosen = jnp.zeros((n_sel, tq), jnp.bool_)
    for _ in range(SEL_TOPK):
        mx = jnp.max(imp, axis=0, keepdims=True)
        first = jnp.min(jnp.where(imp == mx, j, n_sel), axis=0, keepdims=True)
        hit = j == first
        chosen = chosen | hit
        imp = jnp.where(hit, -jnp.inf, imp)
    sel_ref[0, 0] = (chosen & (j <= t_blk)).astype(F32)


def _cmp_attention(qt, cn, ct, slopes, ov):
    b, _, s = qt.shape
    tq = min(Q_TILE, s)
    nq = s // tq
    nc = cn.shape[2]
    n_sel = s // SEL_BLOCK
    lanes = HEADS_PER_GROUP * tq
    rows = HEADS_PER_GROUP * HEAD_DIM
    return pl.pallas_call(
        functools.partial(_cmp_kernel, tq=tq),
        grid=(b, KV_GROUPS, nq),
        in_specs=[
            pl.BlockSpec((1, rows, tq), lambda bi, g, i: (bi, g, i)),
            pl.BlockSpec((1, 1, nc, HEAD_DIM), lambda bi, g, i: (g, bi, 0, 0)),
            pl.BlockSpec((1, 1, HEAD_DIM, nc), lambda bi, g, i: (KV_GROUPS + g, bi, 0, 0)),
            pl.BlockSpec((1, 1, lanes), lambda bi, g, i: (g, 0, 0)),
            _const_spec((n_sel, nc)),
        ],
        out_specs=(
            pl.BlockSpec((1, rows, tq), lambda bi, g, i: (bi, g, i)),
            pl.BlockSpec((1, 1, n_sel, tq), lambda bi, g, i: (bi, g, 0, i)),
        ),
        out_shape=(
            jax.ShapeDtypeStruct((b, NSA_WIDTH, s), F32),
            jax.ShapeDtypeStruct((b, KV_GROUPS, n_sel, s), F32),
        ),
        compiler_params=_params(("parallel", "parallel", "parallel"), 24 << 20),
        name="nsa_cmp_topk",
    )(qt, cn, ct, slopes, ov)


def _online_update(s, v, carry):
    m, l, acc = carry
    m_new = jnp.maximum(m, jnp.max(s, axis=0, keepdims=True))
    e = jnp.exp(s - m_new)
    alpha = jnp.exp(m - m_new)
    l = alpha * l + jnp.sum(e, axis=0, keepdims=True)
    acc = alpha * acc + _dot(v, e.astype(BF16))
    return m_new, l, acc


def _selwin_kernel(q_ref, ks_ref, kw_ref, vs_ref, vw_ref, sel_ref, oc_ref, g_ref, slope_ref, o_ref, *, tq):
    g = pl.program_id(1)
    i = pl.program_id(2)
    lanes = HEADS_PER_GROUP * tq
    tk = KEY_TILE
    half = SEL_BLOCK
    q = _heads_on_lanes(q_ref[0])
    slope = slope_ref[0]
    row = lax.broadcasted_iota(jnp.int32, (tk, lanes), 0)
    qoff = lax.broadcasted_iota(jnp.int32, (tk, lanes), 1) & (tq - 1)
    row_bias = row.astype(F32) * slope
    rel = qoff - row
    t0 = i * tq

    def tile4(v):
        return jnp.concatenate([v] * HEADS_PER_GROUP, axis=1)

    init = (jnp.full((1, lanes), NEG, F32), jnp.zeros((1, lanes), F32), jnp.zeros((HEAD_DIM, lanes), F32))


    def sel_body(jj, carry):
        m, l, acc = carry
        k0 = pl.multiple_of(jj * tk, tk)
        s = _dot(ks_ref[0, 0, pl.ds(k0, tk), :], q) + row_bias
        base = slope * (k0 - t0).astype(F32)
        ca = base + tile4((sel_ref[0, 0, pl.ds(2 * jj, 1), :] - 1.0) * (-NEG))
        cb = base + tile4((sel_ref[0, 0, pl.ds(2 * jj + 1, 1), :] - 1.0) * (-NEG))
        sa = s[:half]
        sb = s[half:]
        m_new = jnp.maximum(m, jnp.maximum(jnp.max(sa, axis=0, keepdims=True) + ca,
                                           jnp.max(sb, axis=0, keepdims=True) + cb))
        ea = jnp.exp(sa - (m_new - ca))
        eb = jnp.exp(sb - (m_new - cb))
        alpha = jnp.exp(m - m_new)
        l = alpha * l + jnp.sum(ea, axis=0, keepdims=True) + jnp.sum(eb, axis=0, keepdims=True)
        p = jnp.concatenate([ea, eb], axis=0).astype(BF16)
        acc = alpha * acc + _dot(vs_ref[0, :, pl.ds(k0, tk)], p)
        return m_new, l, acc

    carry_s = lax.fori_loop(0, i, sel_body, init)

    k0 = pl.multiple_of(t0, tk)
    sel_a = tile4(sel_ref[0, 0, pl.ds(2 * i, 1), :])
    sel_b = tile4(sel_ref[0, 0, pl.ds(2 * i + 1, 1), :])
    keep = (rel >= 0) & (jnp.where(row < half, sel_a, sel_b) > 0.5)
    s = _dot(ks_ref[0, 0, pl.ds(k0, tk), :], q) + row_bias
    m_s, l_s, acc_s = _online_update(jnp.where(keep, s, NEG), vs_ref[0, :, pl.ds(k0, tk)], carry_s)

    def win_body(jj, carry):
        k0 = pl.multiple_of(jj * tk, tk)
        shift = t0 - k0
        dist = rel + shift
        keep = (dist >= 0) & (dist < WINDOW)
        s = _dot(kw_ref[0, 0, pl.ds(k0, tk), :], q) + row_bias - slope * shift.astype(F32)
        return _online_update(jnp.where(keep, s, NEG), vw_ref[0, :, pl.ds(k0, tk)], carry)

    m_w, l_w, acc_w = lax.fori_loop(jnp.maximum(i - WINDOW // tk, 0), i + 1, win_body, init)

    o_s = acc_s / l_s
    o_w = acc_w / l_w
    outs = []
    for r in range(HEADS_PER_GROUP):
        grow = (g * HEADS_PER_GROUP + r) * N_BRANCH
        g_c = g_ref[0, pl.ds(grow, 1), :]
        g_s = g_ref[0, pl.ds(grow + 1, 1), :]
        g_w = g_ref[0, pl.ds(grow + 2, 1), :]
        outs.append(g_c * oc_ref[0, r * HEAD_DIM:(r + 1) * HEAD_DIM, :]
                    + g_s * o_s[:, r * tq:(r + 1) * tq]
                    + g_w * o_w[:, r * tq:(r + 1) * tq])
    o_ref[0] = jnp.concatenate(outs, axis=0).astype(BF16)


def _selwin_attention(qt, kn, vt, selm, ocmp, gt, slopes):
    b, _, s = qt.shape
    tq = min(Q_TILE, s)
    assert tq == 2 * SEL_BLOCK and KEY_TILE == tq and s % tq == 0
    nq = s // tq
    n_sel = s // SEL_BLOCK
    lanes = HEADS_PER_GROUP * tq
    rows = HEADS_PER_GROUP * HEAD_DIM
    return pl.pallas_call(
        functools.partial(_selwin_kernel, tq=tq),
        grid=(b, KV_GROUPS, nq),
        in_specs=[
            pl.BlockSpec((1, rows, tq), lambda bi, g, i: (bi, g, i)),
            pl.BlockSpec((1, 1, s, HEAD_DIM), lambda bi, g, i: (g, bi, 0, 0)),
            pl.BlockSpec((1, 1, s, HEAD_DIM), lambda bi, g, i: (KV_GROUPS + g, bi, 0, 0)),
            pl.BlockSpec((1, HEAD_DIM, s), lambda bi, g, i: (bi, g, 0)),
            pl.BlockSpec((1, HEAD_DIM, s), lambda bi, g, i: (bi, KV_GROUPS + g, 0)),
            pl.BlockSpec((1, 1, n_sel, tq), lambda bi, g, i: (bi, g, 0, i)),
            pl.BlockSpec((1, rows, tq), lambda bi, g, i: (bi, g, i)),
            pl.BlockSpec((1, GATE_ROWS, tq), lambda bi, g, i: (bi, 0, i)),
            pl.BlockSpec((1, 1, lanes), lambda bi, g, i: (g, 0, 0)),
        ],
        out_specs=pl.BlockSpec((1, rows, tq), lambda bi, g, i: (bi, g, i)),
        out_shape=jax.ShapeDtypeStruct((b, NSA_WIDTH, s), BF16),
        compiler_params=_params(("parallel", "parallel", "arbitrary"), 32 << 20),
        name="nsa_sel_win",
    )(qt, kn, kn, vt, vt, selm, ocmp, gt, slopes)


def _outproj_kernel(ot_ref, u_ref, halo_ref, x_ref, woa_ref, wob_ref, wp_ref, ps_ref, out_ref, *, tm):
    i = pl.program_id(1)
    u = u_ref[0]
    halo = jnp.where(i > 0, halo_ref[0], 0.0)
    ext = jnp.concatenate([halo, u], axis=0)
    t1 = i * tm + lax.broadcasted_iota(jnp.int32, (tm, 1), 0) + 1
    ys = []
    for gi, w in enumerate(POOL_WINDOWS):
        e = ext[:, gi * POOL_GROUP_DIM:(gi + 1) * POOL_GROUP_DIM]
        acc = e
        width = 1
        while width < w:
            acc = acc + pltpu.roll(acc, width, 0)
            width *= 2
        div = jnp.minimum(t1, w).astype(F32)
        dlt = acc[POOL_HALO:] / div - e[POOL_HALO:]
        ys.append(_dot(dlt.astype(BF16), wp_ref[gi]))
    y = jnp.concatenate(ys, axis=1) * ps_ref[...]
    mixed = lax.dot_general(ot_ref[0], woa_ref[...], TN_DIMS, preferred_element_type=F32)
    out_ref[0] = x_ref[0] + (mixed + _dot(y.astype(BF16), wob_ref[...]))


def _outproj(ot, u, x, woa, wob, wp, ps):
    b, s, d = x.shape
    tm = min(PROJ_ROWS, s)
    hb = tm // POOL_HALO
    return pl.pallas_call(
        functools.partial(_outproj_kernel, tm=tm),
        grid=(b, s // tm),
        in_specs=[
            pl.BlockSpec((1, NSA_WIDTH, tm), lambda bi, i: (bi, 0, i)),
            pl.BlockSpec((1, tm, POOL_WIDTH), lambda bi, i: (bi, i, 0)),
            pl.BlockSpec((1, POOL_HALO, POOL_WIDTH), lambda bi, i: (bi, jnp.maximum(i * hb - 1, 0), 0)),
            pl.BlockSpec((1, tm, d), lambda bi, i: (bi, i, 0)),
            _const_spec(woa.shape),
            _const_spec(wob.shape),
            _const_spec(wp.shape),
            _const_spec(ps.shape),
        ],
        out_specs=pl.BlockSpec((1, tm, d), lambda bi, i: (bi, i, 0)),
        out_shape=jax.ShapeDtypeStruct((b, s, d), F32),
        compiler_params=_params(("parallel", "parallel"), 40 << 20),
        name="nsa_outproj_pool",
    )(ot, u, u, x, woa, wob, wp, ps)


def _glu_kernel(x_ref, g_ref, w_ref, b_ref, z_ref):
    h = _rms(x_ref[...], g_ref[...]).astype(BF16)
    p = _dot(h, w_ref[...]) + b_ref[...]
    d = z_ref.shape[-1]
    z_ref[...] = p[:, :d] * jax.nn.sigmoid(p[:, d:])


def _glu(x2, g, w, bias):
    n, d = x2.shape
    tm = min(PROJ_ROWS, n)
    return pl.pallas_call(
        _glu_kernel,
        grid=(n // tm,),
        in_specs=[
            pl.BlockSpec((tm, d), lambda i: (i, 0)),
            _const_spec((1, d)),
            _const_spec(w.shape),
            _const_spec(bias.shape),
        ],
        out_specs=pl.BlockSpec((tm, d), lambda i: (i, 0)),
        out_shape=jax.ShapeDtypeStruct((n, d), F32),
        compiler_params=_params(("parallel",), 40 << 20),
        name="conv_pw1_glu",
    )(x2, g, w, bias)


def _dwconv_kernel(z_ref, halo_ref, x_ref, wdw_ref, bdw_ref, lng_ref, lnb_ref, w2_ref, b2_ref, out_ref,
                   zbuf, cbuf, *, tm):
    i = pl.program_id(1)
    zbuf[0:CONV_HALO, :] = jnp.where(i > 0, halo_ref[0], 0.0)
    zbuf[CONV_HALO:, :] = z_ref[0]
    d = z_ref.shape[-1]
    first = CONV_HALO - (CONV_WIDTH - 1)

    def col_body(c, carry):
        c0 = pl.multiple_of(c * LANES, LANES)
        acc = jnp.zeros((tm, LANES), F32)
        for k in range(CONV_WIDTH):
            acc = acc + zbuf[first + k:first + k + tm, pl.ds(c0, LANES)] * wdw_ref[k:k + 1, pl.ds(c0, LANES)]
        cbuf[:, pl.ds(c0, LANES)] = acc
        return carry

    lax.fori_loop(0, d // LANES, col_body, 0)
    y = cbuf[...] + bdw_ref[...]
    mu = jnp.mean(y, axis=-1, keepdims=True)
    yc = y - mu
    var = jnp.mean(yc * yc, axis=-1, keepdims=True)
    y = yc * lax.rsqrt(var + 1e-5) * lng_ref[...] + lnb_ref[...]
    y = jax.nn.silu(y).astype(BF16)
    out_ref[0] = x_ref[0] + (_dot(y, w2_ref[...]) + b2_ref[...])


def _dwconv(z, x, wdw, bdw, lng, lnb, w2, b2):
    b, s, d = x.shape
    tm = min(CONV_ROWS, s)
    hb = tm // CONV_HALO
    return pl.pallas_call(
        functools.partial(_dwconv_kernel, tm=tm),
        grid=(b, s // tm),
        in_specs=[
            pl.BlockSpec((1, tm, d), lambda bi, i: (bi, i, 0)),
            pl.BlockSpec((1, CONV_HALO, d), lambda bi, i: (bi, jnp.maximum(i * hb - 1, 0), 0)),
            pl.BlockSpec((1, tm, d), lambda bi, i: (bi, i, 0)),
            _const_spec(wdw.shape),
            _const_spec(bdw.shape),
            _const_spec(lng.shape),
            _const_spec(lnb.shape),
            _const_spec(w2.shape),
            _const_spec(b2.shape),
        ],
        out_specs=pl.BlockSpec((1, tm, d), lambda bi, i: (bi, i, 0)),
        out_shape=jax.ShapeDtypeStruct((b, s, d), F32),
        scratch_shapes=[pltpu.VMEM((CONV_HALO + tm, d), F32), pltpu.VMEM((tm, d), F32)],
        compiler_params=_params(("parallel", "parallel"), 32 << 20),
        name="conv_dw_ln_pw2",
    )(z, z, x, wdw, bdw, lng, lnb, w2, b2)


def _ffn_kernel(x_ref, g_ref, wg_ref, wu_ref, wd_ref, fg_ref, out_ref, *, final):
    x = x_ref[...]
    h = _rms(x, g_ref[...]).astype(BF16)
    acc = jnp.zeros(x.shape, F32)
    for c in range(wg_ref.shape[0]):
        a = _dot(h, wg_ref[c])
        up = _dot(h, wu_ref[c])
        acc = acc + _dot((jax.nn.silu(a) * up).astype(BF16), wd_ref[c])
    y = x + acc
    if final:
        y = _rms(y, fg_ref[...])
    out_ref[...] = y


def _ffn(x2, g, wg, wu, wd, fg, final):
    n, d = x2.shape
    tm = min(PROJ_ROWS, n)
    return pl.pallas_call(
        functools.partial(_ffn_kernel, final=final),
        grid=(n // tm,),
        in_specs=[
            pl.BlockSpec((tm, d), lambda i: (i, 0)),
            _const_spec((1, d)),
            _const_spec(wg.shape),
            _const_spec(wu.shape),
            _const_spec(wd.shape),
            _const_spec((1, d)),
        ],
        out_specs=pl.BlockSpec((tm, d), lambda i: (i, 0)),
        out_shape=jax.ShapeDtypeStruct((n, d), F32),
        compiler_params=_params(("parallel",), 52 << 20),
        name="ffn_swiglu",
    )(x2, g, wg, wu, wd, fg)


def _overlap_matrix(n_sel, n_cmp_rows):
    c0 = np.arange(n_cmp_rows)[None, :] * CMP_STRIDE
    s0 = np.arange(n_sel)[:, None] * SEL_BLOCK
    ov = np.clip(np.minimum(c0 + CMP_BLOCK, s0 + SEL_BLOCK) - np.maximum(c0, s0), 0, None) / CMP_BLOCK
    return jnp.asarray(ov, BF16)


def _alibi_lane_slopes(tq):
    h = np.arange(1, NSA_HEADS + 1, dtype=np.float64)
    sl = np.exp2(-8.0 * h / NSA_HEADS).reshape(KV_GROUPS, HEADS_PER_GROUP)
    return jnp.asarray(np.repeat(sl, tq, axis=1)[:, None, :], F32)


def _split_w_in(w_in):
    d = w_in.shape[0]
    o = 0
    q = w_in[:, o:o + NSA_WIDTH]; o += NSA_WIDTH
    kc = w_in[:, o:o + KV_WIDTH]; o += KV_WIDTH
    vc = w_in[:, o:o + KV_WIDTH]; o += KV_WIDTH
    ks = w_in[:, o:o + KV_WIDTH]; o += KV_WIDTH
    vs = w_in[:, o:o + KV_WIDTH]; o += KV_WIDTH
    kw = w_in[:, o:o + KV_WIDTH]; o += KV_WIDTH
    vw = w_in[:, o:o + KV_WIDTH]; o += KV_WIDTH
    gate = w_in[:, o:o + GATE_WIDTH]; o += GATE_WIDTH
    u = w_in[:, o:]
    wn = jnp.concatenate([kc, vc, ks, kw, u], axis=1).astype(BF16)
    pad = jnp.zeros((d, GATE_ROWS - GATE_WIDTH), w_in.dtype)
    wt = jnp.concatenate([q, vs, vw, gate, pad], axis=1).T.astype(BF16)
    return wn, wt


def _even_mixer(x, g, w_in, pos_k, pos_v, k_w1, k_w2, v_w1, v_w2, w_pool, pool_scale, w_out):
    b, s, d = x.shape
    dh = HEAD_DIM
    half = CMP_STRIDE * dh
    wn, wt = _split_w_in(w_in)
    cmp_in, kn, u, qt, vt, gt = _inproj(x, g.reshape(1, d), wn, wt)

    pos = jnp.stack([pos_k.reshape(2, 1, half), pos_v.reshape(2, 1, half)])
    w1 = jnp.stack([k_w1.reshape(2, half, CMP_HIDDEN), v_w1.reshape(2, half, CMP_HIDDEN)]).astype(BF16)
    w2 = jnp.stack([k_w2, v_w2]).astype(BF16)
    w2t = jnp.stack([k_w2.T, v_w2.T]).astype(BF16)
    cn, ct = _compress(cmp_in, pos, w1, w2, w2t)

    tq = min(Q_TILE, s)
    slopes = _alibi_lane_slopes(tq)
    ov = _overlap_matrix(s // SEL_BLOCK, s // CMP_STRIDE)
    ocmp, selm = _cmp_attention(qt, cn, ct, slopes, ov)
    ot = _selwin_attention(qt, kn, vt, selm, ocmp, gt, slopes)

    woa = w_out[:NSA_WIDTH].astype(BF16)
    wob = w_out[NSA_WIDTH:].astype(BF16)
    return _outproj(ot, u, x, woa, wob, w_pool.astype(BF16), pool_scale.reshape(1, -1))


def _odd_mixer(x, g, w_pw1, b_pw1, w_dw, b_dw, ln_g, ln_b, w_pw2, b_pw2):
    b, s, d = x.shape
    z = _glu(x.reshape(b * s, d), g.reshape(1, d), w_pw1.astype(BF16), b_pw1.reshape(1, -1))
    wdw = jnp.concatenate([w_dw, jnp.zeros((CONV_HALO - CONV_WIDTH, d), w_dw.dtype)], axis=0)
    return _dwconv(z.reshape(b, s, d), x, wdw, b_dw.reshape(1, d), ln_g.reshape(1, d), ln_b.reshape(1, d),
                   w_pw2.astype(BF16), b_pw2.reshape(1, d))


def _ffn_layer(x, g, w_gate, w_up, w_down, final_g, final):
    b, s, d = x.shape
    hid = w_gate.shape[1]
    nch = hid // FFN_CHUNK
    wg = w_gate.reshape(d, nch, FFN_CHUNK).transpose(1, 0, 2).astype(BF16)
    wu = w_up.reshape(d, nch, FFN_CHUNK).transpose(1, 0, 2).astype(BF16)
    wd = w_down.reshape(nch, FFN_CHUNK, d).astype(BF16)
    y = _ffn(x.reshape(b * s, d), g.reshape(1, d), wg, wu, wd, final_g.reshape(1, d), final)
    return y.reshape(b, s, d)


def kernel(x, mix_norm, ffn_norm, nsa_w_in, cmp_pos_k, cmp_pos_v, cmp_k_w1, cmp_k_w2, cmp_v_w1, cmp_v_w2,
           pool_w, pool_scale, mix_w_out, conv_w_pw1, conv_b_pw1, conv_w_dw, conv_b_dw, conv_ln_g, conv_ln_b,
           conv_w_pw2, conv_b_pw2, ffn_w_gate, ffn_w_up, ffn_w_down, final_norm):
    depth = mix_norm.shape[0]
    for layer in range(depth):
        i = layer // 2
        if layer % 2 == 0:
            x = _even_mixer(x, mix_norm[layer], nsa_w_in[i], cmp_pos_k[i], cmp_pos_v[i], cmp_k_w1[i],
                            cmp_k_w2[i], cmp_v_w1[i], cmp_v_w2[i], pool_w[i], pool_scale[i], mix_w_out[i])
        else:
            x = _odd_mixer(x, mix_norm[layer], conv_w_pw1[i], conv_b_pw1[i], conv_w_dw[i], conv_b_dw[i],
                           conv_ln_g[i], conv_ln_b[i], conv_w_pw2[i], conv_b_pw2[i])
        x = _ffn_layer(x, ffn_norm[layer], ffn_w_gate[layer], ffn_w_up[layer], ffn_w_down[layer],
                       final_norm, final=(layer == depth - 1))
    return x
```

```python
import functools

import numpy as np
import jax
import jax.numpy as jnp
from jax import lax
from jax.experimental import pallas as pl
from jax.experimental.pallas import tpu as pltpu

F32 = jnp.float32
BF16 = jnp.bfloat16

NSA_HEADS = 8
HEAD_DIM = 64
KV_GROUPS = 2
HEADS_PER_GROUP = NSA_HEADS // KV_GROUPS
N_BRANCH = 3
CMP_BLOCK = 32
CMP_STRIDE = 16
CMP_HIDDEN = 128
SEL_BLOCK = 64
SEL_TOPK = 8
WINDOW = 512
FORCE_BONUS = 1e4
NEG = -1e30
POOL_WINDOWS = (2, 4, 8, 16)
POOL_GROUP_DIM = 128
POOL_WIDTH = POOL_GROUP_DIM * len(POOL_WINDOWS)
NSA_WIDTH = NSA_HEADS * HEAD_DIM
KV_WIDTH = KV_GROUPS * HEAD_DIM
GATE_WIDTH = NSA_HEADS * N_BRANCH
GATE_ROWS = 32
CONV_WIDTH = 31
CONV_HALO = 32
POOL_HALO = 16

V7X_VMEM_BYTES = 64 * 1024 * 1024
LANES = 128
SUBLANES = 8

Q_TILE = 128
CMP_Q_TILE = 256
KEY_TILE = 128
SEL_GROUP = 4
PROJ_ROWS = 512
CONV_ROWS = 256
CONV_ROW_CHUNK = 128
FFN_CHUNK = 256

LOG2E = 1.4426950408889634
ALIBI_TERMS = 3
K_COLS = 2 * HEAD_DIM
V_ROWS = HEAD_DIM + 16

NT_DIMS = (((1,), (1,)), ((), ()))
TN_DIMS = (((0,), (0,)), ((), ()))


def _params(semantics, vmem_bytes):
    return pltpu.CompilerParams(dimension_semantics=semantics,
                                vmem_limit_bytes=min(int(vmem_bytes), V7X_VMEM_BYTES - (8 << 20)))


def _const_spec(shape):
    nd = len(shape)
    return pl.BlockSpec(shape, lambda *_: (0,) * nd, pipeline_mode=pl.Buffered(1))


def _dot(a, b):
    return jnp.dot(a, b, preferred_element_type=F32)


def _rms(x, g, eps=1e-6):
    ms = jnp.mean(x * x, axis=-1, keepdims=True)
    return x * lax.rsqrt(ms + eps) * g


def _inproj_kernel(x_ref, g_ref, wn_ref, wt_ref, cmp_ref, kn_ref, u_ref, qt_ref, vt_ref, gt_ref):
    h = _rms(x_ref[0], g_ref[...]).astype(BF16)
    pn = _dot(h, wn_ref[...])
    pt = lax.dot_general(wt_ref[...], h, NT_DIMS, preferred_element_type=F32)
    tm = pn.shape[0]
    col = lax.broadcasted_iota(jnp.int32, (tm, HEAD_DIM), 1)
    koff = lax.broadcasted_iota(jnp.int32, (tm, HEAD_DIM), 0) & (KEY_TILE - 1)
    k_aug = jnp.where(col < ALIBI_TERMS, koff, 0).astype(F32).astype(BF16)
    sub = lax.broadcasted_iota(jnp.int32, (V_ROWS - HEAD_DIM, tm), 0)
    v_aug = jnp.where(sub == 0, 1.0, 0.0).astype(BF16)
    for c in range(2 * KV_GROUPS):
        cmp_ref[c, 0] = pn[:, c * HEAD_DIM:(c + 1) * HEAD_DIM]
        k = pn[:, 2 * KV_WIDTH + c * HEAD_DIM:2 * KV_WIDTH + (c + 1) * HEAD_DIM].astype(BF16)
        kn_ref[c, 0] = jnp.concatenate([k, k_aug], axis=1)
        v = pt[NSA_WIDTH + c * HEAD_DIM:NSA_WIDTH + (c + 1) * HEAD_DIM].astype(BF16)
        vt_ref[0, c * V_ROWS:(c + 1) * V_ROWS, :] = jnp.concatenate([v, v_aug], axis=0)
    u_ref[0] = pn[:, 4 * KV_WIDTH:]
    qt_ref[0] = (pt[:NSA_WIDTH] * (HEAD_DIM ** -0.5 * LOG2E)).astype(BF16)
    gt_ref[0] = jax.nn.sigmoid(pt[NSA_WIDTH + 2 * KV_WIDTH:])


def _inproj(x, g, wn, wt):
    b, s, d = x.shape
    tm = min(PROJ_ROWS, s)
    n_norm = wn.shape[1]
    n_tr = wt.shape[0]
    grid = (b, s // tm)
    out_shape = (
        jax.ShapeDtypeStruct((2 * KV_GROUPS, b, s, HEAD_DIM), F32),
        jax.ShapeDtypeStruct((2 * KV_GROUPS, b, s, K_COLS), BF16),
        jax.ShapeDtypeStruct((b, s, POOL_WIDTH), F32),
        jax.ShapeDtypeStruct((b, NSA_WIDTH, s), BF16),
        jax.ShapeDtypeStruct((b, 2 * KV_GROUPS * V_ROWS, s), BF16),
        jax.ShapeDtypeStruct((b, GATE_ROWS, s), F32),
    )
    return pl.pallas_call(
        _inproj_kernel,
        grid=grid,
        in_specs=[
            pl.BlockSpec((1, tm, d), lambda bi, i: (bi, i, 0)),
            _const_spec((1, d)),
            _const_spec((d, n_norm)),
            _const_spec((n_tr, d)),
        ],
        out_specs=(
            pl.BlockSpec((2 * KV_GROUPS, 1, tm, HEAD_DIM), lambda bi, i: (0, bi, i, 0)),
            pl.BlockSpec((2 * KV_GROUPS, 1, tm, K_COLS), lambda bi, i: (0, bi, i, 0)),
            pl.BlockSpec((1, tm, POOL_WIDTH), lambda bi, i: (bi, i, 0)),
            pl.BlockSpec((1, NSA_WIDTH, tm), lambda bi, i: (bi, 0, i)),
            pl.BlockSpec((1, 2 * KV_GROUPS * V_ROWS, tm), lambda bi, i: (bi, 0, i)),
            pl.BlockSpec((1, GATE_ROWS, tm), lambda bi, i: (bi, 0, i)),
        ),
        out_shape=out_shape,
        compiler_params=_params(("parallel", "parallel"), 40 << 20),
        name="nsa_inproj",
    )(x, g, wn, wt)


def _compress_kernel(sub_ref, pos_ref, w1_ref, w2_ref, w2t_ref, cn_ref, ct_ref):
    sub = sub_ref[0, 0]
    a = _dot((sub + pos_ref[0, 0]).astype(BF16), w1_ref[0, 0])
    bb = _dot((sub + pos_ref[0, 1]).astype(BF16), w1_ref[0, 1])
    nrow = sub.shape[0]
    hid = jax.nn.gelu(a + pltpu.roll(bb, nrow - 1, 0), approximate=True).astype(BF16)
    cn_ref[0, 0] = _dot(hid, w2_ref[0]).astype(BF16)
    ct_ref[0, 0] = lax.dot_general(w2t_ref[0], hid, NT_DIMS, preferred_element_type=F32).astype(BF16)


def _compress(cmp_in, pos, w1, w2, w2t):
    nc4, b, s, dh = cmp_in.shape
    nsub = s // CMP_STRIDE
    sub = cmp_in.reshape(nc4, b, nsub, CMP_STRIDE * dh)
    half = CMP_STRIDE * dh
    return pl.pallas_call(
        _compress_kernel,
        grid=(nc4, b),
        in_specs=[
            pl.BlockSpec((1, 1, nsub, half), lambda c, bi: (c, bi, 0, 0)),
            pl.BlockSpec((1, 2, 1, half), lambda c, bi: (c // KV_GROUPS, 0, 0, 0)),
            pl.BlockSpec((1, 2, half, CMP_HIDDEN), lambda c, bi: (c // KV_GROUPS, 0, 0, 0)),
            pl.BlockSpec((1, CMP_HIDDEN, dh), lambda c, bi: (c // KV_GROUPS, 0, 0)),
            pl.BlockSpec((1, dh, CMP_HIDDEN), lambda c, bi: (c // KV_GROUPS, 0, 0)),
        ],
        out_specs=(
            pl.BlockSpec((1, 1, nsub, dh), lambda c, bi: (c, bi, 0, 0)),
            pl.BlockSpec((1, 1, dh, nsub), lambda c, bi: (c, bi, 0, 0)),
        ),
        out_shape=(
            jax.ShapeDtypeStruct((nc4, b, nsub, dh), BF16),
            jax.ShapeDtypeStruct((nc4, b, dh, nsub), BF16),
        ),
        compiler_params=_params(("parallel", "parallel"), 24 << 20),
        name="nsa_compress",
    )(sub, pos, w1, w2, w2t)


def _heads_on_lanes(qt):
    return jnp.concatenate([qt[r * HEAD_DIM:(r + 1) * HEAD_DIM] for r in range(HEADS_PER_GROUP)], axis=1)


def _heads_on_rows(o, tq):
    return jnp.concatenate([o[:, r * tq:(r + 1) * tq] for r in range(HEADS_PER_GROUP)], axis=0)


def _cmp_kernel(q_ref, kc_ref, vct_ref, slope_ref, ov_ref, o_ref, sel_ref, cnt_ref, *, tq):
    i = pl.program_id(2)
    lanes = HEADS_PER_GROUP * tq
    q = _heads_on_lanes(q_ref[0])
    s = _dot(kc_ref[0, 0], q)
    nc = s.shape[0]
    blk_end = lax.broadcasted_iota(jnp.int32, (nc, lanes), 0) * CMP_STRIDE + (CMP_BLOCK - 1)
    t = i * tq + (lax.broadcasted_iota(jnp.int32, (nc, lanes), 1) & (tq - 1))
    dist = (t - blk_end).astype(F32)
    valid = dist >= 0
    logits = jnp.where(valid, s - slope_ref[0] * dist, NEG)
    m = jnp.max(logits, axis=0, keepdims=True)
    e = jnp.where(valid, jnp.exp2(logits - m), 0.0)
    l = jnp.sum(e, axis=0, keepdims=True)
    p = e / jnp.where(l > 0, l, 1.0)
    o = _dot(vct_ref[0, 0], p.astype(BF16))
    o_ref[0] = _heads_on_rows(o, tq)

    psum = p[:, 0:tq]
    for r in range(1, HEADS_PER_GROUP):
        psum = psum + p[:, r * tq:(r + 1) * tq]
    hi = psum.astype(BF16)
    rem = psum - hi.astype(F32)
    mid = rem.astype(BF16)
    lo = (rem - mid.astype(F32)).astype(BF16)
    ov = ov_ref[...]
    imp = _dot(ov, hi) + _dot(ov, mid) + _dot(ov, lo)

    n_sel = imp.shape[0]
    j = lax.broadcasted_iota(jnp.int32, (n_sel, tq), 0)
    t_blk = (i * tq + lax.broadcasted_iota(jnp.int32, (n_sel, tq), 1)) // SEL_BLOCK
    forced = (j == 0) | (j == t_blk) | (j == t_blk - 1)
    imp = jnp.where(forced, imp + FORCE_BONUS, imp)
    imp = jnp.where(j > t_blk, NEG, imp)
    chosen = jnp.zeros((n_sel, tq), jnp.bool_)
    for _ in range(SEL_TOPK):
        mx = jnp.max(imp, axis=0, keepdims=True)
        first = jnp.min(jnp.where(imp == mx, j, n_sel), axis=0, keepdims=True)
        hit = j == first
        chosen = chosen | hit
        imp = jnp.where(hit, -jnp.inf, imp)
    sel = (chosen & (j <= t_blk)).astype(F32)
    sel_ref[0, 0] = sel
    ones = jnp.ones((8, Q_TILE), BF16)
    for sub in range(tq // Q_TILE):
        part = sel[:, sub * Q_TILE:(sub + 1) * Q_TILE].astype(BF16)
        cnt_ref[0, 0, sub] = lax.dot_general(ones, part, NT_DIMS, preferred_element_type=F32).astype(jnp.int32)


def _cmp_attention(qt, cn, ct, slopes, ov):
    b, _, s = qt.shape
    tq = min(CMP_Q_TILE, s)
    nq = s // tq
    sub_tiles = tq // Q_TILE
    nc = cn.shape[2]
    n_sel = s // SEL_BLOCK
    lanes = HEADS_PER_GROUP * tq
    rows = HEADS_PER_GROUP * HEAD_DIM
    return pl.pallas_call(
        functools.partial(_cmp_kernel, tq=tq),
        grid=(b, KV_GROUPS, nq),
        in_specs=[
            pl.BlockSpec((1, rows, tq), lambda bi, g, i: (bi, g, i)),
            pl.BlockSpec((1, 1, nc, HEAD_DIM), lambda bi, g, i: (g, bi, 0, 0)),
            pl.BlockSpec((1, 1, HEAD_DIM, nc), lambda bi, g, i: (KV_GROUPS + g, bi, 0, 0)),
            pl.BlockSpec((1, 1, lanes), lambda bi, g, i: (g, 0, 0)),
            _const_spec((n_sel, nc)),
        ],
        out_specs=(
            pl.BlockSpec((1, rows, tq), lambda bi, g, i: (bi, g, i)),
            pl.BlockSpec((1, 1, n_sel, tq), lambda bi, g, i: (bi, g, 0, i)),
            pl.BlockSpec((1, 1, sub_tiles, 8, n_sel), lambda bi, g, i: (bi, g, i, 0, 0)),
        ),
        out_shape=(
            jax.ShapeDtypeStruct((b, NSA_WIDTH, s), F32),
            jax.ShapeDtypeStruct((b, KV_GROUPS, n_sel, s), F32),
            jax.ShapeDtypeStruct((b, KV_GROUPS, s // Q_TILE, 8, n_sel), jnp.int32),
        ),
        compiler_params=_params(("parallel", "parallel", "parallel"), 24 << 20),
        name="nsa_cmp_topk",
    )(qt, cn, ct, slopes, ov)


def _softmax_tile(pieces, consts, v):
    tmax = None
    for s, c in zip(pieces, consts):
        t = jnp.max(s, axis=0, keepdims=True) + c
        tmax = t if tmax is None else jnp.maximum(tmax, t)
    p = jnp.concatenate([jnp.exp2(s - (tmax - c)).astype(BF16) for s, c in zip(pieces, consts)], axis=0)
    return tmax, _dot(v, p)


def _softmax_merge(parts):
    m_new = parts[0][0]
    for m, _ in parts[1:]:
        m_new = jnp.maximum(m_new, m)
    acc = None
    for m, o in parts:
        term = jnp.exp2(m - m_new) * o
        acc = term if acc is None else acc + term
    return m_new, acc


def _selwin_kernel(cnt_ref, q_ref, qaug_ref, ks_ref, kw_ref, vs_ref, vw_ref, sel_ref, oc_ref, g_ref, slope_ref,
                   o_ref, act_ref, *, tq, group):
    g = pl.program_id(1)
    i = pl.program_id(2)
    nq = act_ref.shape[0]
    n_sel = sel_ref.shape[2]

    def clear(p, c):
        act_ref[p] = 0
        return c

    lax.fori_loop(0, act_ref.shape[0], clear, 0)
    cnt_base = ((pl.program_id(0) * KV_GROUPS + g) * nq + i) * n_sel

    def scan(jj, n):
        hit = (cnt_ref[cnt_base + 2 * jj] + cnt_ref[cnt_base + 2 * jj + 1]) > 0

        @pl.when(hit)
        def _():
            act_ref[n] = jj

        return n + hit.astype(jnp.int32)

    n_act = lax.fori_loop(0, i, scan, 0)

    lanes = HEADS_PER_GROUP * tq
    tk = KEY_TILE
    half = SEL_BLOCK
    nwin = WINDOW // tk
    q = jnp.concatenate([_heads_on_lanes(q_ref[0]), qaug_ref[0]], axis=0)
    slope = slope_ref[0]
    row = lax.broadcasted_iota(jnp.int32, (tk, lanes), 0)
    qoff = lax.broadcasted_iota(jnp.int32, (tk, lanes), 1) & (tq - 1)
    rel = qoff - row
    t0 = i * tq
    zero_c = jnp.zeros((1, lanes), F32)

    def tile4(v):
        return jnp.concatenate([v] * HEADS_PER_GROUP, axis=1)

    def sel_tiles(first):
        tiles = []
        for u in range(group):
            pos = first + u
            jj = act_ref[jnp.minimum(pos, nq - 1)]
            k0 = pl.multiple_of(jj * tk, tk)
            tiles.append((pos, jj, k0, _dot(ks_ref[0, 0, pl.ds(k0, tk), :], q)))

        def finish():
            parts = []
            for pos, jj, k0, s in tiles:
                off = slope * (k0 - t0).astype(F32) + jnp.where(pos < n_act, 0.0, NEG)
                pieces, consts = [], []
                for hb in range(2):
                    sel_row = sel_ref[0, 0, pl.ds(2 * jj + hb, 1), :]
                    pieces.append(s[hb * half:(hb + 1) * half])
                    consts.append(off + tile4((sel_row - 1.0) * (-NEG)))
                parts.append(_softmax_tile(pieces, consts, vs_ref[0, :, pl.ds(k0, tk)]))
            return parts

        return finish

    k_diag = pl.multiple_of(t0, tk)
    s_diag = _dot(ks_ref[0, 0, pl.ds(k_diag, tk), :], q)
    first_sel = sel_tiles(0)
    win_k0 = [pl.multiple_of(jnp.maximum(i - nwin + u, 0) * tk, tk) for u in range(nwin + 1)]
    win_logits = [_dot(kw_ref[0, 0, pl.ds(k0, tk), :], q) for k0 in win_k0]

    sel_a = tile4(sel_ref[0, 0, pl.ds(2 * i, 1), :])
    sel_b = tile4(sel_ref[0, 0, pl.ds(2 * i + 1, 1), :])
    keep = (rel >= 0) & (jnp.where(row < half, sel_a, sel_b) > 0.5)
    diag = _softmax_tile([jnp.where(keep, s_diag, NEG)], [zero_c], vs_ref[0, :, pl.ds(k_diag, tk)])
    carry_s = _softmax_merge([diag] + first_sel())

    parts = []
    for u in range(nwin + 1):
        s = win_logits[u]
        if u == 0:
            s = jnp.where(rel < 0, s, NEG)
        if u == nwin:
            s = jnp.where(rel >= 0, s, NEG)
        c = slope * float((u - nwin) * tk) + jnp.where(i - nwin + u >= 0, 0.0, NEG)
        parts.append(_softmax_tile([s], [c], vw_ref[0, :, pl.ds(win_k0[u], tk)]))
    _, acc_w = _softmax_merge(parts)

    def sel_group(gi, carry):
        return _softmax_merge([carry] + sel_tiles(gi * group)())

    _, acc_s = lax.fori_loop(1, (n_act + group - 1) // group, sel_group, carry_s)

    o_s = acc_s[:HEAD_DIM] / acc_s[HEAD_DIM:HEAD_DIM + 1]
    o_w = acc_w[:HEAD_DIM] / acc_w[HEAD_DIM:HEAD_DIM + 1]
    outs = []
    for r in range(HEADS_PER_GROUP):
        grow = (g * HEADS_PER_GROUP + r) * N_BRANCH
        g_c = g_ref[0, pl.ds(grow, 1), :]
        g_s = g_ref[0, pl.ds(grow + 1, 1), :]
        g_w = g_ref[0, pl.ds(grow + 2, 1), :]
        outs.append(g_c * oc_ref[0, r * HEAD_DIM:(r + 1) * HEAD_DIM, :]
                    + g_s * o_s[:, r * tq:(r + 1) * tq]
                    + g_w * o_w[:, r * tq:(r + 1) * tq])
    o_ref[0] = jnp.concatenate(outs, axis=0).astype(BF16)


def _selwin_attention(qt, qaug, kn, vt, selm, selcnt, ocmp, gt, slopes):
    b, _, s = qt.shape
    tq = min(Q_TILE, s)
    assert tq == 2 * SEL_BLOCK and KEY_TILE == tq and s % tq == 0 and PROJ_ROWS % KEY_TILE == 0
    nq = s // tq
    n_sel = s // SEL_BLOCK
    lanes = HEADS_PER_GROUP * tq
    rows = HEADS_PER_GROUP * HEAD_DIM
    group = min(SEL_GROUP, nq)
    grid_spec = pltpu.PrefetchScalarGridSpec(
        num_scalar_prefetch=1,
        grid=(b, KV_GROUPS, nq),
        in_specs=[
            pl.BlockSpec((1, rows, tq), lambda bi, g, i, c: (bi, g, i)),
            pl.BlockSpec((1, K_COLS - HEAD_DIM, lanes), lambda bi, g, i, c: (g, 0, 0)),
            pl.BlockSpec((1, 1, s, K_COLS), lambda bi, g, i, c: (g, bi, 0, 0)),
            pl.BlockSpec((1, 1, s, K_COLS), lambda bi, g, i, c: (KV_GROUPS + g, bi, 0, 0)),
            pl.BlockSpec((1, V_ROWS, s), lambda bi, g, i, c: (bi, g, 0)),
            pl.BlockSpec((1, V_ROWS, s), lambda bi, g, i, c: (bi, KV_GROUPS + g, 0)),
            pl.BlockSpec((1, 1, n_sel, tq), lambda bi, g, i, c: (bi, g, 0, i)),
            pl.BlockSpec((1, rows, tq), lambda bi, g, i, c: (bi, g, i)),
            pl.BlockSpec((1, GATE_ROWS, tq), lambda bi, g, i, c: (bi, 0, i)),
            pl.BlockSpec((1, 1, lanes), lambda bi, g, i, c: (g, 0, 0)),
        ],
        out_specs=pl.BlockSpec((1, rows, tq), lambda bi, g, i, c: (bi, g, i)),
        scratch_shapes=[pltpu.SMEM((nq,), jnp.int32)],
    )
    return pl.pallas_call(
        functools.partial(_selwin_kernel, tq=tq, group=group),
        grid_spec=grid_spec,
        out_shape=jax.ShapeDtypeStruct((b, NSA_WIDTH, s), BF16),
        compiler_params=_params(("parallel", "parallel", "arbitrary"), 32 << 20),
        name="nsa_sel_win",
    )(selcnt[:, :, :, 0, :].reshape(-1), qt, qaug, kn, kn, vt, vt, selm, ocmp, gt, slopes)


def _outproj_kernel(ot_ref, u_ref, halo_ref, x_ref, woa_ref, wob_ref, wp_ref, ps_ref, out_ref, *, tm):
    i = pl.program_id(1)
    u = u_ref[0]
    halo = jnp.where(i > 0, halo_ref[0], 0.0)
    ext = jnp.concatenate([halo, u], axis=0)
    t1 = i * tm + lax.broadcasted_iota(jnp.int32, (tm, 1), 0) + 1
    ys = []
    for gi, w in enumerate(POOL_WINDOWS):
        e = ext[:, gi * POOL_GROUP_DIM:(gi + 1) * POOL_GROUP_DIM]
        acc = e
        width = 1
        while width < w:
            acc = acc + pltpu.roll(acc, width, 0)
            width *= 2
        div = jnp.minimum(t1, w).astype(F32)
        dlt = acc[POOL_HALO:] / div - e[POOL_HALO:]
        ys.append(_dot(dlt.astype(BF16), wp_ref[gi]))
    y = jnp.concatenate(ys, axis=1) * ps_ref[...]
    mixed = lax.dot_general(ot_ref[0], woa_ref[...], TN_DIMS, preferred_element_type=F32)
    out_ref[0] = x_ref[0] + (mixed + _dot(y.astype(BF16), wob_ref[...]))


def _outproj(ot, u, x, woa, wob, wp, ps):
    b, s, d = x.shape
    tm = min(PROJ_ROWS, s)
    hb = tm // POOL_HALO
    return pl.pallas_call(
        functools.partial(_outproj_kernel, tm=tm),
        grid=(b, s // tm),
        in_specs=[
            pl.BlockSpec((1, NSA_WIDTH, tm), lambda bi, i: (bi, 0, i)),
            pl.BlockSpec((1, tm, POOL_WIDTH), lambda bi, i: (bi, i, 0)),
            pl.BlockSpec((1, POOL_HALO, POOL_WIDTH), lambda bi, i: (bi, jnp.maximum(i * hb - 1, 0), 0)),
            pl.BlockSpec((1, tm, d), lambda bi, i: (bi, i, 0)),
            _const_spec(woa.shape),
            _const_spec(wob.shape),
            _const_spec(wp.shape),
            _const_spec(ps.shape),
        ],
        out_specs=pl.BlockSpec((1, tm, d), lambda bi, i: (bi, i, 0)),
        out_shape=jax.ShapeDtypeStruct((b, s, d), F32),
        compiler_params=_params(("parallel", "parallel"), 40 << 20),
        name="nsa_outproj_pool",
    )(ot, u, u, x, woa, wob, wp, ps)


def _glu_kernel(x_ref, g_ref, w_ref, b_ref, z_ref):
    h = _rms(x_ref[...], g_ref[...]).astype(BF16)
    p = _dot(h, w_ref[...]) + b_ref[...]
    d = z_ref.shape[-1]
    z_ref[...] = p[:, :d] * jax.nn.sigmoid(p[:, d:])


def _glu(x2, g, w, bias):
    n, d = x2.shape
    tm = min(PROJ_ROWS, n)
    return pl.pallas_call(
        _glu_kernel,
        grid=(n // tm,),
        in_specs=[
            pl.BlockSpec((tm, d), lambda i: (i, 0)),
            _const_spec((1, d)),
            _const_spec(w.shape),
            _const_spec(bias.shape),
        ],
        out_specs=pl.BlockSpec((tm, d), lambda i: (i, 0)),
        out_shape=jax.ShapeDtypeStruct((n, d), F32),
        compiler_params=_params(("parallel",), 40 << 20),
        name="conv_pw1_glu",
    )(x2, g, w, bias)


def _dwconv_kernel(z_ref, halo_ref, x_ref, wdw_ref, bdw_ref, lng_ref, lnb_ref, w2_ref, b2_ref, out_ref,
                   zbuf, cbuf, *, tm):
    i = pl.program_id(1)
    zbuf[0:CONV_HALO, :] = jnp.where(i > 0, halo_ref[0], 0.0)
    zbuf[CONV_HALO:, :] = z_ref[0]
    d = z_ref.shape[-1]
    first = CONV_HALO - (CONV_WIDTH - 1)

    def col_body(c, carry):
        c0 = pl.multiple_of(c * LANES, LANES)
        for r0 in range(0, tm, CONV_ROW_CHUNK):
            acc = None
            for res in range(SUBLANES):
                off = (first + res) % SUBLANES
                base = r0 + first + res - off
                nrow = CONV_ROW_CHUNK + (SUBLANES if off else 0)
                y = None
                for k in range(res, CONV_WIDTH, SUBLANES):
                    lo = base + k - res
                    term = zbuf[lo:lo + nrow, pl.ds(c0, LANES)] * wdw_ref[k:k + 1, pl.ds(c0, LANES)]
                    y = term if y is None else y + term
                part = y[off:off + CONV_ROW_CHUNK]
                acc = part if acc is None else acc + part
            cbuf[r0:r0 + CONV_ROW_CHUNK, pl.ds(c0, LANES)] = acc
        return carry

    lax.fori_loop(0, d // LANES, col_body, 0)
    y = cbuf[...] + bdw_ref[...]
    mu = jnp.mean(y, axis=-1, keepdims=True)
    yc = y - mu
    var = jnp.mean(yc * yc, axis=-1, keepdims=True)
    y = yc * lax.rsqrt(var + 1e-5) * lng_ref[...] + lnb_ref[...]
    y = jax.nn.silu(y).astype(BF16)
    out_ref[0] = x_ref[0] + (_dot(y, w2_ref[...]) + b2_ref[...])


def _dwconv(z, x, wdw, bdw, lng, lnb, w2, b2):
    b, s, d = x.shape
    tm = min(CONV_ROWS, s)
    hb = tm // CONV_HALO
    return pl.pallas_call(
        functools.partial(_dwconv_kernel, tm=tm),
        grid=(b, s // tm),
        in_specs=[
            pl.BlockSpec((1, tm, d), lambda bi, i: (bi, i, 0)),
            pl.BlockSpec((1, CONV_HALO, d), lambda bi, i: (bi, jnp.maximum(i * hb - 1, 0), 0)),
            pl.BlockSpec((1, tm, d), lambda bi, i: (bi, i, 0)),
            _const_spec(wdw.shape),
            _const_spec(bdw.shape),
            _const_spec(lng.shape),
            _const_spec(lnb.shape),
            _const_spec(w2.shape),
            _const_spec(b2.shape),
        ],
        out_specs=pl.BlockSpec((1, tm, d), lambda bi, i: (bi, i, 0)),
        out_shape=jax.ShapeDtypeStruct((b, s, d), F32),
        scratch_shapes=[pltpu.VMEM((CONV_HALO + tm, d), F32), pltpu.VMEM((tm, d), F32)],
        compiler_params=_params(("parallel", "parallel"), 32 << 20),
        name="conv_dw_ln_pw2",
    )(z, z, x, wdw, bdw, lng, lnb, w2, b2)


def _ffn_kernel(x_ref, g_ref, wg_ref, wu_ref, wd_ref, fg_ref, out_ref, *, final):
    x = x_ref[...]
    h = _rms(x, g_ref[...]).astype(BF16)
    acc = jnp.zeros(x.shape, F32)
    for c in range(wg_ref.shape[0]):
        a = _dot(h, wg_ref[c])
        up = _dot(h, wu_ref[c])
        acc = acc + _dot((jax.nn.silu(a) * up).astype(BF16), wd_ref[c])
    y = x + acc
    if final:
        y = _rms(y, fg_ref[...])
    out_ref[...] = y


def _ffn(x2, g, wg, wu, wd, fg, final):
    n, d = x2.shape
    tm = min(PROJ_ROWS, n)
    return pl.pallas_call(
        functools.partial(_ffn_kernel, final=final),
        grid=(n // tm,),
        in_specs=[
            pl.BlockSpec((tm, d), lambda i: (i, 0)),
            _const_spec((1, d)),
            _const_spec(wg.shape),
            _const_spec(wu.shape),
            _const_spec(wd.shape),
            _const_spec((1, d)),
        ],
        out_specs=pl.BlockSpec((tm, d), lambda i: (i, 0)),
        out_shape=jax.ShapeDtypeStruct((n, d), F32),
        compiler_params=_params(("parallel",), 52 << 20),
        name="ffn_swiglu",
    )(x2, g, wg, wu, wd, fg)


def _overlap_matrix(n_sel, n_cmp_rows):
    c0 = np.arange(n_cmp_rows)[None, :] * CMP_STRIDE
    s0 = np.arange(n_sel)[:, None] * SEL_BLOCK
    ov = np.clip(np.minimum(c0 + CMP_BLOCK, s0 + SEL_BLOCK) - np.maximum(c0, s0), 0, None) / CMP_BLOCK
    return jnp.asarray(ov, BF16)


def _alibi_lane_slopes(tq):
    h = np.arange(1, NSA_HEADS + 1, dtype=np.float64)
    sl = np.exp2(-8.0 * h / NSA_HEADS).reshape(KV_GROUPS, HEADS_PER_GROUP) * LOG2E
    lane = jnp.asarray(np.repeat(sl, tq, axis=1)[:, None, :], F32)
    terms, rem = [], lane
    for _ in range(ALIBI_TERMS):
        t = rem.astype(BF16)
        terms.append(t)
        rem = rem - t.astype(F32)
    pad = jnp.zeros((KV_GROUPS, K_COLS - HEAD_DIM - ALIBI_TERMS, lane.shape[-1]), BF16)
    return lane, jnp.concatenate(terms + [pad], axis=1)


def _split_w_in(w_in):
    d = w_in.shape[0]
    o = 0
    q = w_in[:, o:o + NSA_WIDTH]; o += NSA_WIDTH
    kc = w_in[:, o:o + KV_WIDTH]; o += KV_WIDTH
    vc = w_in[:, o:o + KV_WIDTH]; o += KV_WIDTH
    ks = w_in[:, o:o + KV_WIDTH]; o += KV_WIDTH
    vs = w_in[:, o:o + KV_WIDTH]; o += KV_WIDTH
    kw = w_in[:, o:o + KV_WIDTH]; o += KV_WIDTH
    vw = w_in[:, o:o + KV_WIDTH]; o += KV_WIDTH
    gate = w_in[:, o:o + GATE_WIDTH]; o += GATE_WIDTH
    u = w_in[:, o:]
    wn = jnp.concatenate([kc, vc, ks, kw, u], axis=1).astype(BF16)
    pad = jnp.zeros((d, GATE_ROWS - GATE_WIDTH), w_in.dtype)
    wt = jnp.concatenate([q, vs, vw, gate, pad], axis=1).T.astype(BF16)
    return wn, wt


def _even_mixer(x, g, w_in, pos_k, pos_v, k_w1, k_w2, v_w1, v_w2, w_pool, pool_scale, w_out):
    b, s, d = x.shape
    dh = HEAD_DIM
    half = CMP_STRIDE * dh
    wn, wt = _split_w_in(w_in)
    cmp_in, kn, u, qt, vt, gt = _inproj(x, g.reshape(1, d), wn, wt)

    pos = jnp.stack([pos_k.reshape(2, 1, half), pos_v.reshape(2, 1, half)])
    w1 = jnp.stack([k_w1.reshape(2, half, CMP_HIDDEN), v_w1.reshape(2, half, CMP_HIDDEN)]).astype(BF16)
    w2 = jnp.stack([k_w2, v_w2]).astype(BF16)
    w2t = jnp.stack([k_w2.T, v_w2.T]).astype(BF16)
    cn, ct = _compress(cmp_in, pos, w1, w2, w2t)

    tq = min(Q_TILE, s)
    slopes, qaug = _alibi_lane_slopes(tq)
    ov = _overlap_matrix(s // SEL_BLOCK, s // CMP_STRIDE)
    ocmp, selm, selcnt = _cmp_attention(qt, cn, ct, _alibi_lane_slopes(min(CMP_Q_TILE, s))[0], ov)
    ot = _selwin_attention(qt, qaug, kn, vt, selm, selcnt, ocmp, gt, slopes)

    woa = w_out[:NSA_WIDTH].astype(BF16)
    wob = w_out[NSA_WIDTH:].astype(BF16)
    return _outproj(ot, u, x, woa, wob, w_pool.astype(BF16), pool_scale.reshape(1, -1))


def _odd_mixer(x, g, w_pw1, b_pw1, w_dw, b_dw, ln_g, ln_b, w_pw2, b_pw2):
    b, s, d = x.shape
    z = _glu(x.reshape(b * s, d), g.reshape(1, d), w_pw1.astype(BF16), b_pw1.reshape(1, -1))
    wdw = jnp.concatenate([w_dw, jnp.zeros((CONV_HALO - CONV_WIDTH, d), w_dw.dtype)], axis=0)
    return _dwconv(z.reshape(b, s, d), x, wdw, b_dw.reshape(1, d), ln_g.reshape(1, d), ln_b.reshape(1, d),
                   w_pw2.astype(BF16), b_pw2.reshape(1, d))


def _ffn_layer(x, g, w_gate, w_up, w_down, final_g, final):
    b, s, d = x.shape
    hid = w_gate.shape[1]
    nch = hid // FFN_CHUNK
    wg = w_gate.reshape(d, nch, FFN_CHUNK).transpose(1, 0, 2).astype(BF16)
    wu = w_up.reshape(d, nch, FFN_CHUNK).transpose(1, 0, 2).astype(BF16)
    wd = w_down.reshape(nch, FFN_CHUNK, d).astype(BF16)
    y = _ffn(x.reshape(b * s, d), g.reshape(1, d), wg, wu, wd, final_g.reshape(1, d), final)
    return y.reshape(b, s, d)


def kernel(x, mix_norm, ffn_norm, nsa_w_in, cmp_pos_k, cmp_pos_v, cmp_k_w1, cmp_k_w2, cmp_v_w1, cmp_v_w2,
           pool_w, pool_scale, mix_w_out, conv_w_pw1, conv_b_pw1, conv_w_dw, conv_b_dw, conv_ln_g, conv_ln_b,
           conv_w_pw2, conv_b_pw2, ffn_w_gate, ffn_w_up, ffn_w_down, final_norm):
    depth = mix_norm.shape[0]
    for layer in range(depth):
        i = layer // 2
        if layer % 2 == 0:
            x = _even_mixer(x, mix_norm[layer], nsa_w_in[i], cmp_pos_k[i], cmp_pos_v[i], cmp_k_w1[i],
                            cmp_k_w2[i], cmp_v_w1[i], cmp_v_w2[i], pool_w[i], pool_scale[i], mix_w_out[i])
        else:
            x = _odd_mixer(x, mix_norm[layer], conv_w_pw1[i], conv_b_pw1[i], conv_w_dw[i], conv_b_dw[i],
                           conv_ln_g[i], conv_ln_b[i], conv_w_pw2[i], conv_b_pw2[i])
        x = _ffn_layer(x, ffn_norm[layer], ffn_w_gate[layer], ffn_w_up[layer], ffn_w_down[layer],
                       final_norm, final=(layer == depth - 1))
    return x
```

```python
import functools

import numpy as np
import jax
import jax.numpy as jnp
from jax import lax
from jax.experimental import pallas as pl
from jax.experimental.pallas import tpu as pltpu

F32 = jnp.float32
BF16 = jnp.bfloat16

NSA_HEADS = 8
HEAD_DIM = 64
KV_GROUPS = 2
HEADS_PER_GROUP = NSA_HEADS // KV_GROUPS
N_BRANCH = 3
CMP_BLOCK = 32
CMP_STRIDE = 16
CMP_HIDDEN = 128
SEL_BLOCK = 64
SEL_TOPK = 8
WINDOW = 512
FORCE_BONUS = 1e4
NEG = -1e30
POOL_WINDOWS = (2, 4, 8, 16)
POOL_GROUP_DIM = 128
POOL_WIDTH = POOL_GROUP_DIM * len(POOL_WINDOWS)
NSA_WIDTH = NSA_HEADS * HEAD_DIM
KV_WIDTH = KV_GROUPS * HEAD_DIM
GATE_WIDTH = NSA_HEADS * N_BRANCH
GATE_ROWS = 32
CONV_WIDTH = 31
CONV_HALO = 32
POOL_HALO = 16

V7X_VMEM_BYTES = 64 * 1024 * 1024
LANES = 128
SUBLANES = 8

Q_TILE = 128
CMP_Q_TILE = 256
KEY_TILE = 128
SEL_FIRST_SLOTS = 6
SEL_LOOP_SLOTS = 2
TILES_PER_SOFTMAX = 2
PROJ_ROWS = 512
CONV_ROWS = 512
CONV_ROW_CHUNK = 128
FFN_CHUNK = 256

LOG2E = 1.4426950408889634
ALIBI_TERMS = 3
K_COLS = 2 * HEAD_DIM
V_ROWS = HEAD_DIM + 16

NT_DIMS = (((1,), (1,)), ((), ()))
TN_DIMS = (((0,), (0,)), ((), ()))


def _params(semantics, vmem_bytes):
    return pltpu.CompilerParams(dimension_semantics=semantics,
                                vmem_limit_bytes=min(int(vmem_bytes), V7X_VMEM_BYTES - (8 << 20)))


def _const_spec(shape):
    nd = len(shape)
    return pl.BlockSpec(shape, lambda *_: (0,) * nd, pipeline_mode=pl.Buffered(1))


def _dot(a, b):
    return jnp.dot(a, b, preferred_element_type=F32)


def _rms(x, g, eps=1e-6):
    ms = jnp.mean(x * x, axis=-1, keepdims=True)
    return x * lax.rsqrt(ms + eps) * g


def _inproj_kernel(x_ref, g_ref, wn_ref, wt_ref, cmp_ref, kn_ref, u_ref, qt_ref, vt_ref, gt_ref):
    h = _rms(x_ref[0], g_ref[...]).astype(BF16)
    pn = _dot(h, wn_ref[...])
    pt = lax.dot_general(wt_ref[...], h, NT_DIMS, preferred_element_type=F32)
    tm = pn.shape[0]
    col = lax.broadcasted_iota(jnp.int32, (tm, HEAD_DIM), 1)
    koff = lax.broadcasted_iota(jnp.int32, (tm, HEAD_DIM), 0) & (KEY_TILE - 1)
    k_aug = jnp.where(col < ALIBI_TERMS, koff, 0).astype(F32).astype(BF16)
    sub = lax.broadcasted_iota(jnp.int32, (V_ROWS - HEAD_DIM, tm), 0)
    v_aug = jnp.where(sub == 0, 1.0, 0.0).astype(BF16)
    for c in range(2 * KV_GROUPS):
        cmp_ref[c, 0] = pn[:, c * HEAD_DIM:(c + 1) * HEAD_DIM]
        k = pn[:, 2 * KV_WIDTH + c * HEAD_DIM:2 * KV_WIDTH + (c + 1) * HEAD_DIM].astype(BF16)
        kn_ref[c, 0] = jnp.concatenate([k, k_aug], axis=1)
        v = pt[NSA_WIDTH + c * HEAD_DIM:NSA_WIDTH + (c + 1) * HEAD_DIM].astype(BF16)
        vt_ref[0, c * V_ROWS:(c + 1) * V_ROWS, :] = jnp.concatenate([v, v_aug], axis=0)
    u_ref[0] = pn[:, 4 * KV_WIDTH:]
    qt_ref[0] = (pt[:NSA_WIDTH] * (HEAD_DIM ** -0.5 * LOG2E)).astype(BF16)
    gt_ref[0] = jax.nn.sigmoid(pt[NSA_WIDTH + 2 * KV_WIDTH:])


def _inproj(x, g, wn, wt):
    b, s, d = x.shape
    tm = min(PROJ_ROWS, s)
    n_norm = wn.shape[1]
    n_tr = wt.shape[0]
    grid = (b, s // tm)
    out_shape = (
        jax.ShapeDtypeStruct((2 * KV_GROUPS, b, s, HEAD_DIM), F32),
        jax.ShapeDtypeStruct((2 * KV_GROUPS, b, s, K_COLS), BF16),
        jax.ShapeDtypeStruct((b, s, POOL_WIDTH), F32),
        jax.ShapeDtypeStruct((b, NSA_WIDTH, s), BF16),
        jax.ShapeDtypeStruct((b, 2 * KV_GROUPS * V_ROWS, s), BF16),
        jax.ShapeDtypeStruct((b, GATE_ROWS, s), F32),
    )
    return pl.pallas_call(
        _inproj_kernel,
        grid=grid,
        in_specs=[
            pl.BlockSpec((1, tm, d), lambda bi, i: (bi, i, 0)),
            _const_spec((1, d)),
            _const_spec((d, n_norm)),
            _const_spec((n_tr, d)),
        ],
        out_specs=(
            pl.BlockSpec((2 * KV_GROUPS, 1, tm, HEAD_DIM), lambda bi, i: (0, bi, i, 0)),
            pl.BlockSpec((2 * KV_GROUPS, 1, tm, K_COLS), lambda bi, i: (0, bi, i, 0)),
            pl.BlockSpec((1, tm, POOL_WIDTH), lambda bi, i: (bi, i, 0)),
            pl.BlockSpec((1, NSA_WIDTH, tm), lambda bi, i: (bi, 0, i)),
            pl.BlockSpec((1, 2 * KV_GROUPS * V_ROWS, tm), lambda bi, i: (bi, 0, i)),
            pl.BlockSpec((1, GATE_ROWS, tm), lambda bi, i: (bi, 0, i)),
        ),
        out_shape=out_shape,
        compiler_params=_params(("parallel", "parallel"), 40 << 20),
        name="nsa_inproj",
    )(x, g, wn, wt)


def _compress_kernel(sub_ref, pos_ref, w1_ref, w2_ref, w2t_ref, cn_ref, ct_ref):
    sub = sub_ref[0, 0]
    a = _dot((sub + pos_ref[0, 0]).astype(BF16), w1_ref[0, 0])
    bb = _dot((sub + pos_ref[0, 1]).astype(BF16), w1_ref[0, 1])
    nrow = sub.shape[0]
    hid = jax.nn.gelu(a + pltpu.roll(bb, nrow - 1, 0), approximate=True).astype(BF16)
    n_aug = cn_ref.shape[-1] - HEAD_DIM
    n = lax.broadcasted_iota(jnp.int32, (nrow, n_aug), 0)
    col = lax.broadcasted_iota(jnp.int32, (nrow, n_aug), 1)
    end_hi = n // (KEY_TILE // CMP_STRIDE)
    end_lo = (n % (KEY_TILE // CMP_STRIDE)) * CMP_STRIDE + (CMP_BLOCK - 1)
    aug = jnp.where(col < ALIBI_TERMS, end_hi, jnp.where(col < 2 * ALIBI_TERMS, end_lo, 0))
    cn_ref[0, 0] = jnp.concatenate([_dot(hid, w2_ref[0]).astype(BF16), aug.astype(F32).astype(BF16)], axis=1)
    sub_row = lax.broadcasted_iota(jnp.int32, (V_ROWS - HEAD_DIM, nrow), 0)
    ones_row = jnp.where(sub_row == 0, 1.0, 0.0).astype(BF16)
    vt = lax.dot_general(w2t_ref[0], hid, NT_DIMS, preferred_element_type=F32).astype(BF16)
    ct_ref[0, 0] = jnp.concatenate([vt, ones_row], axis=0)


def _compress(cmp_in, pos, w1, w2, w2t):
    nc4, b, s, dh = cmp_in.shape
    nsub = s // CMP_STRIDE
    sub = cmp_in.reshape(nc4, b, nsub, CMP_STRIDE * dh)
    half = CMP_STRIDE * dh
    return pl.pallas_call(
        _compress_kernel,
        grid=(nc4, b),
        in_specs=[
            pl.BlockSpec((1, 1, nsub, half), lambda c, bi: (c, bi, 0, 0)),
            pl.BlockSpec((1, 2, 1, half), lambda c, bi: (c // KV_GROUPS, 0, 0, 0)),
            pl.BlockSpec((1, 2, half, CMP_HIDDEN), lambda c, bi: (c // KV_GROUPS, 0, 0, 0)),
            pl.BlockSpec((1, CMP_HIDDEN, dh), lambda c, bi: (c // KV_GROUPS, 0, 0)),
            pl.BlockSpec((1, dh, CMP_HIDDEN), lambda c, bi: (c // KV_GROUPS, 0, 0)),
        ],
        out_specs=(
            pl.BlockSpec((1, 1, nsub, K_COLS), lambda c, bi: (c, bi, 0, 0)),
            pl.BlockSpec((1, 1, V_ROWS, nsub), lambda c, bi: (c, bi, 0, 0)),
        ),
        out_shape=(
            jax.ShapeDtypeStruct((nc4, b, nsub, K_COLS), BF16),
            jax.ShapeDtypeStruct((nc4, b, V_ROWS, nsub), BF16),
        ),
        compiler_params=_params(("parallel", "parallel"), 24 << 20),
        name="nsa_compress",
    )(sub, pos, w1, w2, w2t)


def _heads_on_lanes(qt):
    return jnp.concatenate([qt[r * HEAD_DIM:(r + 1) * HEAD_DIM] for r in range(HEADS_PER_GROUP)], axis=1)


def _heads_on_rows(o, tq):
    return jnp.concatenate([o[:, r * tq:(r + 1) * tq] for r in range(HEADS_PER_GROUP)], axis=0)


def _cmp_kernel(q_ref, qaug_ref, kc_ref, vct_ref, ov_ref, o_ref, sel_ref, cnt_ref, *, tq):
    i = pl.program_id(2)
    lanes = HEADS_PER_GROUP * tq
    q = jnp.concatenate([_heads_on_lanes(q_ref[0]), qaug_ref[0]], axis=0)
    s = _dot(kc_ref[0, 0], q)
    nc = s.shape[0]
    n = lax.broadcasted_iota(jnp.int32, (nc, lanes), 0)
    t_lane = i * tq + (lax.broadcasted_iota(jnp.int32, (1, lanes), 1) & (tq - 1))
    last = lax.shift_right_arithmetic(t_lane - (CMP_BLOCK - 1), CMP_STRIDE.bit_length() - 1)
    logits = jnp.where(n <= last, s, NEG)
    m = jnp.max(logits, axis=0, keepdims=True)
    e = jnp.exp2(logits - m).astype(BF16)
    n_sel = ov_ref.shape[0]
    prod = _dot(jnp.concatenate([vct_ref[0, 0], ov_ref[...]], axis=0), e)
    inv = jnp.where(last >= 0, 1.0 / prod[HEAD_DIM:HEAD_DIM + 1], 0.0)
    o_ref[0] = _heads_on_rows(prod[:HEAD_DIM] * inv, tq)
    imp_heads = prod[V_ROWS:] * inv
    imp = imp_heads[:, 0:tq]
    for r in range(1, HEADS_PER_GROUP):
        imp = imp + imp_heads[:, r * tq:(r + 1) * tq]

    j = lax.broadcasted_iota(jnp.int32, (n_sel, tq), 0)
    t_blk = (i * tq + lax.broadcasted_iota(jnp.int32, (n_sel, tq), 1)) // SEL_BLOCK
    forced = (j == 0) | (j == t_blk) | (j == t_blk - 1)
    imp = jnp.where(forced, imp + FORCE_BONUS, imp)
    imp = jnp.where(j > t_blk, NEG, imp)
    chosen = jnp.zeros((n_sel, tq), jnp.bool_)
    for _ in range(SEL_TOPK):
        mx = jnp.max(imp, axis=0, keepdims=True)
        first = jnp.min(jnp.where(imp == mx, j, n_sel), axis=0, keepdims=True)
        hit = j == first
        chosen = chosen | hit
        imp = jnp.where(hit, -jnp.inf, imp)
    sel = (chosen & (j <= t_blk)).astype(F32)
    sel_ref[0, 0] = sel
    ones = jnp.ones((8, Q_TILE), BF16)
    for sub in range(tq // Q_TILE):
        part = sel[:, sub * Q_TILE:(sub + 1) * Q_TILE].astype(BF16)
        cnt_ref[0, 0, sub] = lax.dot_general(ones, part, NT_DIMS, preferred_element_type=F32).astype(jnp.int32)


def _cmp_attention(qt, qaug, cn, ct, ov):
    b, _, s = qt.shape
    tq = min(CMP_Q_TILE, s)
    nq = s // tq
    sub_tiles = tq // Q_TILE
    nc = cn.shape[2]
    n_sel = s // SEL_BLOCK
    lanes = HEADS_PER_GROUP * tq
    rows = HEADS_PER_GROUP * HEAD_DIM
    return pl.pallas_call(
        functools.partial(_cmp_kernel, tq=tq),
        grid=(b, KV_GROUPS, nq),
        in_specs=[
            pl.BlockSpec((1, rows, tq), lambda bi, g, i: (bi, g, i)),
            pl.BlockSpec((1, K_COLS - HEAD_DIM, lanes), lambda bi, g, i: (g, 0, 0)),
            pl.BlockSpec((1, 1, nc, K_COLS), lambda bi, g, i: (g, bi, 0, 0)),
            pl.BlockSpec((1, 1, V_ROWS, nc), lambda bi, g, i: (KV_GROUPS + g, bi, 0, 0)),
            _const_spec((n_sel, nc)),
        ],
        out_specs=(
            pl.BlockSpec((1, rows, tq), lambda bi, g, i: (bi, g, i)),
            pl.BlockSpec((1, 1, n_sel, tq), lambda bi, g, i: (bi, g, 0, i)),
            pl.BlockSpec((1, 1, sub_tiles, 8, n_sel), lambda bi, g, i: (bi, g, i, 0, 0)),
        ),
        out_shape=(
            jax.ShapeDtypeStruct((b, NSA_WIDTH, s), F32),
            jax.ShapeDtypeStruct((b, KV_GROUPS, n_sel, s), F32),
            jax.ShapeDtypeStruct((b, KV_GROUPS, s // Q_TILE, 8, n_sel), jnp.int32),
        ),
        compiler_params=_params(("parallel", "parallel", "parallel"), 24 << 20),
        name="nsa_cmp_topk",
    )(qt, qaug, cn, ct, ov)


def _softmax_tile(pieces, consts, vs):
    tmax = None
    for s, c in zip(pieces, consts):
        t = jnp.max(s, axis=0, keepdims=True) + c
        tmax = t if tmax is None else jnp.maximum(tmax, t)
    p = jnp.concatenate([jnp.exp2(s - (tmax - c)).astype(BF16) for s, c in zip(pieces, consts)], axis=0)
    v = vs[0] if len(vs) == 1 else jnp.concatenate(vs, axis=1)
    return tmax, _dot(v, p)


def _softmax_merge(parts):
    m_new = parts[0][0]
    for m, _ in parts[1:]:
        m_new = jnp.maximum(m_new, m)
    acc = None
    for m, o in parts:
        term = jnp.exp2(m - m_new) * o
        acc = term if acc is None else acc + term
    return m_new, acc


def _selwin_kernel(cnt_ref, q_ref, qaug_ref, ks_ref, kw_ref, vs_ref, vw_ref, sel_ref, oc_ref, g_ref, slope_ref,
                   o_ref, act_ref, *, tq, nq):
    g = pl.program_id(1)
    i = pl.program_id(2)
    n_sel = sel_ref.shape[2]

    cnt_base = ((pl.program_id(0) * KV_GROUPS + g) * nq + i) * n_sel
    scan_unroll = next(c for c in (4, 2, 1) if nq % c == 0)

    def scan(c, n):
        for u in range(scan_unroll):
            jj = c * scan_unroll + u
            hit = (jj < i) & ((cnt_ref[cnt_base + 2 * jj] + cnt_ref[cnt_base + 2 * jj + 1]) > 0)
            act_ref[n] = jj
            n = n + hit.astype(jnp.int32)
        return n

    n_act = lax.fori_loop(0, nq // scan_unroll, scan, 0)
    for k in range(act_ref.shape[0] - nq):
        act_ref[n_act + k] = 0

    lanes = HEADS_PER_GROUP * tq
    tk = KEY_TILE
    half = SEL_BLOCK
    nwin = WINDOW // tk
    q = jnp.concatenate([_heads_on_lanes(q_ref[0]), qaug_ref[0]], axis=0)
    slope = slope_ref[0]
    row = lax.broadcasted_iota(jnp.int32, (tk, lanes), 0)
    qoff = lax.broadcasted_iota(jnp.int32, (tk, lanes), 1) & (tq - 1)
    rel = qoff - row
    t0 = i * tq
    zero_c = jnp.zeros((1, lanes), F32)

    def tile4(v):
        return jnp.concatenate([v] * HEADS_PER_GROUP, axis=1)

    def sel_tiles(first, nslots):
        tiles = []
        for u in range(nslots):
            pos = first + u
            jj = act_ref[pos]
            k0 = pl.multiple_of(jj * tk, tk)
            tiles.append((pos, jj, k0, _dot(ks_ref[0, 0, pl.ds(k0, tk), :], q)))

        def finish():
            parts = []
            for a in range(0, nslots, TILES_PER_SOFTMAX):
                pieces, consts, vs = [], [], []
                for pos, jj, k0, s in tiles[a:a + TILES_PER_SOFTMAX]:
                    off = slope * (k0 - t0).astype(F32) + jnp.where(pos < n_act, 0.0, NEG)
                    for hb in range(2):
                        sel_row = sel_ref[0, 0, pl.ds(2 * jj + hb, 1), :]
                        pieces.append(s[hb * half:(hb + 1) * half])
                        consts.append(off + tile4((sel_row - 1.0) * (-NEG)))
                    vs.append(vs_ref[0, :, pl.ds(k0, tk)])
                parts.append(_softmax_tile(pieces, consts, vs))
            return parts

        return finish

    k_diag = pl.multiple_of(t0, tk)
    s_diag = _dot(ks_ref[0, 0, pl.ds(k_diag, tk), :], q)
    first_sel = sel_tiles(0, SEL_FIRST_SLOTS)
    win_k0 = [pl.multiple_of(jnp.maximum(i - nwin + u, 0) * tk, tk) for u in range(nwin + 1)]
    win_logits = [_dot(kw_ref[0, 0, pl.ds(k0, tk), :], q) for k0 in win_k0]

    sel_a = tile4(sel_ref[0, 0, pl.ds(2 * i, 1), :])
    sel_b = tile4(sel_ref[0, 0, pl.ds(2 * i + 1, 1), :])
    keep = (rel >= 0) & (jnp.where(row < half, sel_a, sel_b) > 0.5)
    diag = _softmax_tile([jnp.where(keep, s_diag, NEG)], [zero_c], [vs_ref[0, :, pl.ds(k_diag, tk)]])
    carry_s = _softmax_merge([diag] + first_sel())

    parts = []
    for a in range(0, nwin + 1, TILES_PER_SOFTMAX):
        pieces, consts, vs = [], [], []
        for u in range(a, min(a + TILES_PER_SOFTMAX, nwin + 1)):
            s = win_logits[u]
            if u == 0:
                s = jnp.where(rel < 0, s, NEG)
            if u == nwin:
                s = jnp.where(rel >= 0, s, NEG)
            pieces.append(s)
            consts.append(slope * float((u - nwin) * tk) + jnp.where(i - nwin + u >= 0, 0.0, NEG))
            vs.append(vw_ref[0, :, pl.ds(win_k0[u], tk)])
        parts.append(_softmax_tile(pieces, consts, vs))
    _, acc_w = _softmax_merge(parts)

    def sel_more(gi, carry):
        return _softmax_merge([carry] + sel_tiles(SEL_FIRST_SLOTS + gi * SEL_LOOP_SLOTS, SEL_LOOP_SLOTS)())

    n_more = (n_act - SEL_FIRST_SLOTS + SEL_LOOP_SLOTS - 1) // SEL_LOOP_SLOTS
    _, acc_s = lax.fori_loop(0, n_more, sel_more, carry_s)

    o_s = acc_s[:HEAD_DIM] / acc_s[HEAD_DIM:HEAD_DIM + 1]
    o_w = acc_w[:HEAD_DIM] / acc_w[HEAD_DIM:HEAD_DIM + 1]
    outs = []
    for r in range(HEADS_PER_GROUP):
        grow = (g * HEADS_PER_GROUP + r) * N_BRANCH
        g_c = g_ref[0, pl.ds(grow, 1), :]
        g_s = g_ref[0, pl.ds(grow + 1, 1), :]
        g_w = g_ref[0, pl.ds(grow + 2, 1), :]
        outs.append(g_c * oc_ref[0, r * HEAD_DIM:(r + 1) * HEAD_DIM, :]
                    + g_s * o_s[:, r * tq:(r + 1) * tq]
                    + g_w * o_w[:, r * tq:(r + 1) * tq])
    o_ref[0] = jnp.concatenate(outs, axis=0).astype(BF16)


def _selwin_attention(qt, qaug, kn, vt, selm, selcnt, ocmp, gt, slopes):
    b, _, s = qt.shape
    tq = min(Q_TILE, s)
    assert tq == 2 * SEL_BLOCK and KEY_TILE == tq and s % tq == 0 and PROJ_ROWS % KEY_TILE == 0
    nq = s // tq
    n_sel = s // SEL_BLOCK
    lanes = HEADS_PER_GROUP * tq
    rows = HEADS_PER_GROUP * HEAD_DIM
    spare_slots = max(SEL_FIRST_SLOTS, SEL_LOOP_SLOTS)
    grid_spec = pltpu.PrefetchScalarGridSpec(
        num_scalar_prefetch=1,
        grid=(b, KV_GROUPS, nq),
        in_specs=[
            pl.BlockSpec((1, rows, tq), lambda bi, g, i, c: (bi, g, i)),
            pl.BlockSpec((1, K_COLS - HEAD_DIM, lanes), lambda bi, g, i, c: (g, 0, 0)),
            pl.BlockSpec((1, 1, s, K_COLS), lambda bi, g, i, c: (g, bi, 0, 0)),
            pl.BlockSpec((1, 1, s, K_COLS), lambda bi, g, i, c: (KV_GROUPS + g, bi, 0, 0)),
            pl.BlockSpec((1, V_ROWS, s), lambda bi, g, i, c: (bi, g, 0)),
            pl.BlockSpec((1, V_ROWS, s), lambda bi, g, i, c: (bi, KV_GROUPS + g, 0)),
            pl.BlockSpec((1, 1, n_sel, tq), lambda bi, g, i, c: (bi, g, 0, i)),
            pl.BlockSpec((1, rows, tq), lambda bi, g, i, c: (bi, g, i)),
            pl.BlockSpec((1, GATE_ROWS, tq), lambda bi, g, i, c: (bi, 0, i)),
            pl.BlockSpec((1, 1, lanes), lambda bi, g, i, c: (g, 0, 0)),
        ],
        out_specs=pl.BlockSpec((1, rows, tq), lambda bi, g, i, c: (bi, g, i)),
        scratch_shapes=[pltpu.SMEM((nq + spare_slots,), jnp.int32)],
    )
    return pl.pallas_call(
        functools.partial(_selwin_kernel, tq=tq, nq=nq),
        grid_spec=grid_spec,
        out_shape=jax.ShapeDtypeStruct((b, NSA_WIDTH, s), BF16),
        compiler_params=_params(("parallel", "parallel", "arbitrary"), 32 << 20),
        name="nsa_sel_win",
    )(selcnt[:, :, :, 0, :].reshape(-1), qt, qaug, kn, kn, vt, vt, selm, ocmp, gt, slopes)


def _outproj_kernel(ot_ref, u_ref, halo_ref, x_ref, woa_ref, wob_ref, wp_ref, ps_ref, out_ref, *, tm):
    i = pl.program_id(1)
    u = u_ref[0]
    halo = jnp.where(i > 0, halo_ref[0], 0.0)
    ext = jnp.concatenate([halo, u], axis=0)
    t1 = i * tm + lax.broadcasted_iota(jnp.int32, (tm, 1), 0) + 1
    ys = []
    for gi, w in enumerate(POOL_WINDOWS):
        e = ext[:, gi * POOL_GROUP_DIM:(gi + 1) * POOL_GROUP_DIM]
        acc = e
        width = 1
        while width < w:
            acc = acc + pltpu.roll(acc, width, 0)
            width *= 2
        div = jnp.minimum(t1, w).astype(F32)
        dlt = acc[POOL_HALO:] / div - e[POOL_HALO:]
        ys.append(_dot(dlt.astype(BF16), wp_ref[gi]))
    y = jnp.concatenate(ys, axis=1) * ps_ref[...]
    mixed = lax.dot_general(ot_ref[0], woa_ref[...], TN_DIMS, preferred_element_type=F32)
    out_ref[0] = x_ref[0] + (mixed + _dot(y.astype(BF16), wob_ref[...]))


def _outproj(ot, u, x, woa, wob, wp, ps):
    b, s, d = x.shape
    tm = min(PROJ_ROWS, s)
    hb = tm // POOL_HALO
    return pl.pallas_call(
        functools.partial(_outproj_kernel, tm=tm),
        grid=(b, s // tm),
        in_specs=[
            pl.BlockSpec((1, NSA_WIDTH, tm), lambda bi, i: (bi, 0, i)),
            pl.BlockSpec((1, tm, POOL_WIDTH), lambda bi, i: (bi, i, 0)),
            pl.BlockSpec((1, POOL_HALO, POOL_WIDTH), lambda bi, i: (bi, jnp.maximum(i * hb - 1, 0), 0)),
            pl.BlockSpec((1, tm, d), lambda bi, i: (bi, i, 0)),
            _const_spec(woa.shape),
            _const_spec(wob.shape),
            _const_spec(wp.shape),
            _const_spec(ps.shape),
        ],
        out_specs=pl.BlockSpec((1, tm, d), lambda bi, i: (bi, i, 0)),
        out_shape=jax.ShapeDtypeStruct((b, s, d), F32),
        compiler_params=_params(("parallel", "parallel"), 40 << 20),
        name="nsa_outproj_pool",
    )(ot, u, u, x, woa, wob, wp, ps)


def _glu_kernel(x_ref, g_ref, w_ref, b_ref, z_ref):
    h = _rms(x_ref[...], g_ref[...]).astype(BF16)
    p = _dot(h, w_ref[...]) + b_ref[...]
    d = z_ref.shape[-1]
    z_ref[...] = p[:, :d] * jax.nn.sigmoid(p[:, d:])


def _glu(x2, g, w, bias):
    n, d = x2.shape
    tm = min(PROJ_ROWS, n)
    return pl.pallas_call(
        _glu_kernel,
        grid=(n // tm,),
        in_specs=[
            pl.BlockSpec((tm, d), lambda i: (i, 0)),
            _const_spec((1, d)),
            _const_spec(w.shape),
            _const_spec(bias.shape),
        ],
        out_specs=pl.BlockSpec((tm, d), lambda i: (i, 0)),
        out_shape=jax.ShapeDtypeStruct((n, d), F32),
        compiler_params=_params(("parallel",), 40 << 20),
        name="conv_pw1_glu",
    )(x2, g, w, bias)


def _dwconv_kernel(z_ref, halo_ref, x_ref, wdw_ref, bdw_ref, lng_ref, lnb_ref, w2_ref, b2_ref, out_ref,
                   zbuf, cbuf, *, tm):
    i = pl.program_id(1)
    zbuf[0:CONV_HALO, :] = jnp.where(i > 0, halo_ref[0], 0.0)
    zbuf[CONV_HALO:, :] = z_ref[0]
    d = z_ref.shape[-1]
    first = CONV_HALO - (CONV_WIDTH - 1)

    def col_body(c, carry):
        c0 = pl.multiple_of(c * LANES, LANES)
        for r0 in range(0, tm, CONV_ROW_CHUNK):
            acc = None
            for res in range(SUBLANES):
                off = (first + res) % SUBLANES
                base = r0 + first + res - off
                nrow = CONV_ROW_CHUNK + (SUBLANES if off else 0)
                y = None
                for k in range(res, CONV_WIDTH, SUBLANES):
                    lo = base + k - res
                    term = zbuf[lo:lo + nrow, pl.ds(c0, LANES)] * wdw_ref[k:k + 1, pl.ds(c0, LANES)]
                    y = term if y is None else y + term
                part = y[off:off + CONV_ROW_CHUNK]
                acc = part if acc is None else acc + part
            cbuf[r0:r0 + CONV_ROW_CHUNK, pl.ds(c0, LANES)] = acc
        return carry

    lax.fori_loop(0, d // LANES, col_body, 0)
    y = cbuf[...] + bdw_ref[...]
    mu = jnp.mean(y, axis=-1, keepdims=True)
    yc = y - mu
    var = jnp.mean(yc * yc, axis=-1, keepdims=True)
    y = yc * lax.rsqrt(var + 1e-5) * lng_ref[...] + lnb_ref[...]
    y = jax.nn.silu(y).astype(BF16)
    out_ref[0] = x_ref[0] + (_dot(y, w2_ref[...]) + b2_ref[...])


def _dwconv(z, x, wdw, bdw, lng, lnb, w2, b2):
    b, s, d = x.shape
    tm = min(CONV_ROWS, s)
    hb = tm // CONV_HALO
    return pl.pallas_call(
        functools.partial(_dwconv_kernel, tm=tm),
        grid=(b, s // tm),
        in_specs=[
            pl.BlockSpec((1, tm, d), lambda bi, i: (bi, i, 0)),
            pl.BlockSpec((1, CONV_HALO, d), lambda bi, i: (bi, jnp.maximum(i * hb - 1, 0), 0)),
            pl.BlockSpec((1, tm, d), lambda bi, i: (bi, i, 0)),
            _const_spec(wdw.shape),
            _const_spec(bdw.shape),
            _const_spec(lng.shape),
            _const_spec(lnb.shape),
            _const_spec(w2.shape),
            _const_spec(b2.shape),
        ],
        out_specs=pl.BlockSpec((1, tm, d), lambda bi, i: (bi, i, 0)),
        out_shape=jax.ShapeDtypeStruct((b, s, d), F32),
        scratch_shapes=[pltpu.VMEM((CONV_HALO + tm, d), F32), pltpu.VMEM((tm, d), F32)],
        compiler_params=_params(("parallel", "parallel"), 32 << 20),
        name="conv_dw_ln_pw2",
    )(z, z, x, wdw, bdw, lng, lnb, w2, b2)


def _ffn_kernel(x_ref, g_ref, wg_ref, wu_ref, wd_ref, fg_ref, out_ref, *, final):
    x = x_ref[...]
    h = _rms(x, g_ref[...]).astype(BF16)
    acc = jnp.zeros(x.shape, F32)
    for c in range(wd_ref.shape[0]):
        cols = slice(c * FFN_CHUNK, (c + 1) * FFN_CHUNK)
        a = _dot(h, wg_ref[:, cols])
        up = _dot(h, wu_ref[:, cols])
        acc = acc + _dot((jax.nn.silu(a) * up).astype(BF16), wd_ref[c])
    y = x + acc
    if final:
        y = _rms(y, fg_ref[...])
    out_ref[...] = y


def _ffn(x2, g, wg, wu, wd, fg, final):
    n, d = x2.shape
    tm = min(PROJ_ROWS, n)
    return pl.pallas_call(
        functools.partial(_ffn_kernel, final=final),
        grid=(n // tm,),
        in_specs=[
            pl.BlockSpec((tm, d), lambda i: (i, 0)),
            _const_spec((1, d)),
            _const_spec(wg.shape),
            _const_spec(wu.shape),
            _const_spec(wd.shape),
            _const_spec((1, d)),
        ],
        out_specs=pl.BlockSpec((tm, d), lambda i: (i, 0)),
        out_shape=jax.ShapeDtypeStruct((n, d), F32),
        compiler_params=_params(("parallel",), 52 << 20),
        name="ffn_swiglu",
    )(x2, g, wg, wu, wd, fg)


def _overlap_matrix(n_sel, n_cmp_rows):
    c0 = np.arange(n_cmp_rows)[None, :] * CMP_STRIDE
    s0 = np.arange(n_sel)[:, None] * SEL_BLOCK
    ov = np.clip(np.minimum(c0 + CMP_BLOCK, s0 + SEL_BLOCK) - np.maximum(c0, s0), 0, None) / CMP_BLOCK
    return jnp.asarray(ov, BF16)


def _alibi_lane_slopes(tq):
    h = np.arange(1, NSA_HEADS + 1, dtype=np.float64)
    sl = np.exp2(-8.0 * h / NSA_HEADS).reshape(KV_GROUPS, HEADS_PER_GROUP) * LOG2E
    lane = jnp.asarray(np.repeat(sl, tq, axis=1)[:, None, :], F32)
    terms, rem = [], lane
    for _ in range(ALIBI_TERMS):
        t = rem.astype(BF16)
        terms.append(t)
        rem = rem - t.astype(F32)
    pad = jnp.zeros((KV_GROUPS, K_COLS - HEAD_DIM - ALIBI_TERMS, lane.shape[-1]), BF16)
    hi_lo = [t * float(KEY_TILE) for t in terms] + terms
    return lane, jnp.concatenate(terms + [pad], axis=1), jnp.concatenate(hi_lo + [pad[:, ALIBI_TERMS:]], axis=1)


def _split_w_in(w_in):
    d = w_in.shape[0]
    o = 0
    q = w_in[:, o:o + NSA_WIDTH]; o += NSA_WIDTH
    kc = w_in[:, o:o + KV_WIDTH]; o += KV_WIDTH
    vc = w_in[:, o:o + KV_WIDTH]; o += KV_WIDTH
    ks = w_in[:, o:o + KV_WIDTH]; o += KV_WIDTH
    vs = w_in[:, o:o + KV_WIDTH]; o += KV_WIDTH
    kw = w_in[:, o:o + KV_WIDTH]; o += KV_WIDTH
    vw = w_in[:, o:o + KV_WIDTH]; o += KV_WIDTH
    gate = w_in[:, o:o + GATE_WIDTH]; o += GATE_WIDTH
    u = w_in[:, o:]
    wn = jnp.concatenate([kc, vc, ks, kw, u], axis=1).astype(BF16)
    pad = jnp.zeros((d, GATE_ROWS - GATE_WIDTH), w_in.dtype)
    wt = jnp.concatenate([q, vs, vw, gate, pad], axis=1).T.astype(BF16)
    return wn, wt


def _even_mixer(x, g, w_in, pos_k, pos_v, k_w1, k_w2, v_w1, v_w2, w_pool, pool_scale, w_out):
    b, s, d = x.shape
    dh = HEAD_DIM
    half = CMP_STRIDE * dh
    wn, wt = _split_w_in(w_in)
    cmp_in, kn, u, qt, vt, gt = _inproj(x, g.reshape(1, d), wn, wt)

    pos = jnp.stack([pos_k.reshape(2, 1, half), pos_v.reshape(2, 1, half)])
    w1 = jnp.stack([k_w1.reshape(2, half, CMP_HIDDEN), v_w1.reshape(2, half, CMP_HIDDEN)]).astype(BF16)
    w2 = jnp.stack([k_w2, v_w2]).astype(BF16)
    w2t = jnp.stack([k_w2.T, v_w2.T]).astype(BF16)
    cn, ct = _compress(cmp_in, pos, w1, w2, w2t)

    tq = min(Q_TILE, s)
    slopes, qaug, _ = _alibi_lane_slopes(tq)
    ov = _overlap_matrix(s // SEL_BLOCK, s // CMP_STRIDE)
    ocmp, selm, selcnt = _cmp_attention(qt, _alibi_lane_slopes(min(CMP_Q_TILE, s))[2], cn, ct, ov)
    ot = _selwin_attention(qt, qaug, kn, vt, selm, selcnt, ocmp, gt, slopes)

    woa = w_out[:NSA_WIDTH].astype(BF16)
    wob = w_out[NSA_WIDTH:].astype(BF16)
    return _outproj(ot, u, x, woa, wob, w_pool.astype(BF16), pool_scale.reshape(1, -1))


def _odd_mixer(x, g, w_pw1, b_pw1, w_dw, b_dw, ln_g, ln_b, w_pw2, b_pw2):
    b, s, d = x.shape
    z = _glu(x.reshape(b * s, d), g.reshape(1, d), w_pw1.astype(BF16), b_pw1.reshape(1, -1))
    wdw = jnp.concatenate([w_dw, jnp.zeros((CONV_HALO - CONV_WIDTH, d), w_dw.dtype)], axis=0)
    return _dwconv(z.reshape(b, s, d), x, wdw, b_dw.reshape(1, d), ln_g.reshape(1, d), ln_b.reshape(1, d),
                   w_pw2.astype(BF16), b_pw2.reshape(1, d))


def _ffn_layer(x, g, w_gate, w_up, w_down, final_g, final):
    b, s, d = x.shape
    hid = w_gate.shape[1]
    nch = hid // FFN_CHUNK
    wg = w_gate.astype(BF16)
    wu = w_up.astype(BF16)
    wd = w_down.reshape(nch, FFN_CHUNK, d).astype(BF16)
    y = _ffn(x.reshape(b * s, d), g.reshape(1, d), wg, wu, wd, final_g.reshape(1, d), final)
    return y.reshape(b, s, d)


def kernel(x, mix_norm, ffn_norm, nsa_w_in, cmp_pos_k, cmp_pos_v, cmp_k_w1, cmp_k_w2, cmp_v_w1, cmp_v_w2,
           pool_w, pool_scale, mix_w_out, conv_w_pw1, conv_b_pw1, conv_w_dw, conv_b_dw, conv_ln_g, conv_ln_b,
           conv_w_pw2, conv_b_pw2, ffn_w_gate, ffn_w_up, ffn_w_down, final_norm):
    depth = mix_norm.shape[0]
    for layer in range(depth):
        i = layer // 2
        if layer % 2 == 0:
            x = _even_mixer(x, mix_norm[layer], nsa_w_in[i], cmp_pos_k[i], cmp_pos_v[i], cmp_k_w1[i],
                            cmp_k_w2[i], cmp_v_w1[i], cmp_v_w2[i], pool_w[i], pool_scale[i], mix_w_out[i])
        else:
            x = _odd_mixer(x, mix_norm[layer], conv_w_pw1[i], conv_b_pw1[i], conv_w_dw[i], conv_b_dw[i],
                           conv_ln_g[i], conv_ln_b[i], conv_w_pw2[i], conv_b_pw2[i])
        x = _ffn_layer(x, ffn_norm[layer], ffn_w_gate[layer], ffn_w_up[layer], ffn_w_down[layer],
                       final_norm, final=(layer == depth - 1))
    return x
```

```python
import functools

import numpy as np
import jax
import jax.numpy as jnp
from jax import lax
from jax.experimental import pallas as pl
from jax.experimental.pallas import tpu as pltpu

F32 = jnp.float32
BF16 = jnp.bfloat16

NSA_HEADS = 8
HEAD_DIM = 64
KV_GROUPS = 2
HEADS_PER_GROUP = NSA_HEADS // KV_GROUPS
N_BRANCH = 3
CMP_BLOCK = 32
CMP_STRIDE = 16
CMP_HIDDEN = 128
SEL_BLOCK = 64
SEL_TOPK = 8
WINDOW = 512
FORCE_BONUS = 1e4
NEG = -1e30
POOL_WINDOWS = (2, 4, 8, 16)
POOL_GROUP_DIM = 128
POOL_WIDTH = POOL_GROUP_DIM * len(POOL_WINDOWS)
NSA_WIDTH = NSA_HEADS * HEAD_DIM
KV_WIDTH = KV_GROUPS * HEAD_DIM
GATE_WIDTH = NSA_HEADS * N_BRANCH
GATE_ROWS = 32
CONV_WIDTH = 31
CONV_HALO = 32
POOL_HALO = 16

V7X_VMEM_BYTES = 64 * 1024 * 1024
LANES = 128
SUBLANES = 8

Q_TILE = 128
CMP_Q_TILE = 512
KEY_TILE = 128
SEL_FIRST_SLOTS = 5
SEL_LOOP_SLOTS = 2
SEL_STEP_TILES = 2
TILES_PER_SOFTMAX = 2
PROJ_ROWS = 512
CONV_ROWS = 512
CONV_ROW_CHUNK = 128
FFN_ROWS = 1024
FFN_CHUNK = 256

LOG2E = 1.4426950408889634
ALIBI_TERMS = 3
K_COLS = 2 * HEAD_DIM
V_ROWS = HEAD_DIM + 16

NT_DIMS = (((1,), (1,)), ((), ()))
TN_DIMS = (((0,), (0,)), ((), ()))


def _params(semantics, vmem_bytes):
    return pltpu.CompilerParams(dimension_semantics=semantics,
                                vmem_limit_bytes=min(int(vmem_bytes), V7X_VMEM_BYTES - (8 << 20)))


def _const_spec(shape):
    nd = len(shape)
    return pl.BlockSpec(shape, lambda *_: (0,) * nd, pipeline_mode=pl.Buffered(1))


def _dot(a, b):
    return jnp.dot(a, b, preferred_element_type=F32)


def _rms(x, g, eps=1e-6):
    ms = jnp.mean(x * x, axis=-1, keepdims=True)
    return x * lax.rsqrt(ms + eps) * g


def _inproj_kernel(x_ref, g_ref, wn_ref, wt_ref, cmp_ref, kn_ref, u_ref, qt_ref, vt_ref, gt_ref, cmp_buf):
    h = _rms(x_ref[0], g_ref[...]).astype(BF16)
    pn = _dot(h, wn_ref[...])
    pt = lax.dot_general(wt_ref[...], h, NT_DIMS, preferred_element_type=F32)
    tm = pn.shape[0]
    col = lax.broadcasted_iota(jnp.int32, (tm, HEAD_DIM), 1)
    koff = lax.broadcasted_iota(jnp.int32, (tm, HEAD_DIM), 0) & (KEY_TILE - 1)
    k_aug = jnp.where(col < ALIBI_TERMS, koff, 0).astype(F32).astype(BF16)
    sub = lax.broadcasted_iota(jnp.int32, (V_ROWS - HEAD_DIM, tm), 0)
    v_aug = jnp.where(sub == 0, 1.0, 0.0).astype(BF16)
    for c in range(2 * KV_GROUPS):
        k = pn[:, 2 * KV_WIDTH + c * HEAD_DIM:2 * KV_WIDTH + (c + 1) * HEAD_DIM].astype(BF16)
        kn_ref[c, 0] = jnp.concatenate([k, k_aug], axis=1)
        v = pt[NSA_WIDTH + c * HEAD_DIM:NSA_WIDTH + (c + 1) * HEAD_DIM].astype(BF16)
        vt_ref[0, c * V_ROWS:(c + 1) * V_ROWS, :] = jnp.concatenate([v, v_aug], axis=0)
    for kv in range(2):
        cmp_buf[kv] = pn[:, kv * KV_WIDTH:(kv + 1) * KV_WIDTH]
    for j in range(CMP_STRIDE):
        for kv in range(2):
            slab = cmp_buf[kv, pl.ds(j, tm // CMP_STRIDE, stride=CMP_STRIDE), :]
            for g in range(KV_GROUPS):
                cmp_ref[kv * KV_GROUPS + g, 0, :, j * HEAD_DIM:(j + 1) * HEAD_DIM] = (
                    slab[:, g * HEAD_DIM:(g + 1) * HEAD_DIM])
    u_ref[0] = pn[:, 4 * KV_WIDTH:]
    qt_ref[0] = (pt[:NSA_WIDTH] * (HEAD_DIM ** -0.5 * LOG2E)).astype(BF16)
    gt_ref[0] = jax.nn.sigmoid(pt[NSA_WIDTH + 2 * KV_WIDTH:])


def _inproj(x, g, wn, wt):
    b, s, d = x.shape
    tm = min(PROJ_ROWS, s)
    n_norm = wn.shape[1]
    n_tr = wt.shape[0]
    grid = (b, s // tm)
    out_shape = (
        jax.ShapeDtypeStruct((2 * KV_GROUPS, b, s // CMP_STRIDE, CMP_STRIDE * HEAD_DIM), F32),
        jax.ShapeDtypeStruct((2 * KV_GROUPS, b, s, K_COLS), BF16),
        jax.ShapeDtypeStruct((b, s, POOL_WIDTH), F32),
        jax.ShapeDtypeStruct((b, NSA_WIDTH, s), BF16),
        jax.ShapeDtypeStruct((b, 2 * KV_GROUPS * V_ROWS, s), BF16),
        jax.ShapeDtypeStruct((b, GATE_ROWS, s), F32),
    )
    return pl.pallas_call(
        _inproj_kernel,
        grid=grid,
        in_specs=[
            pl.BlockSpec((1, tm, d), lambda bi, i: (bi, i, 0)),
            _const_spec((1, d)),
            _const_spec((d, n_norm)),
            _const_spec((n_tr, d)),
        ],
        out_specs=(
            pl.BlockSpec((2 * KV_GROUPS, 1, tm // CMP_STRIDE, CMP_STRIDE * HEAD_DIM), lambda bi, i: (0, bi, i, 0)),
            pl.BlockSpec((2 * KV_GROUPS, 1, tm, K_COLS), lambda bi, i: (0, bi, i, 0)),
            pl.BlockSpec((1, tm, POOL_WIDTH), lambda bi, i: (bi, i, 0)),
            pl.BlockSpec((1, NSA_WIDTH, tm), lambda bi, i: (bi, 0, i)),
            pl.BlockSpec((1, 2 * KV_GROUPS * V_ROWS, tm), lambda bi, i: (bi, 0, i)),
            pl.BlockSpec((1, GATE_ROWS, tm), lambda bi, i: (bi, 0, i)),
        ),
        out_shape=out_shape,
        scratch_shapes=[pltpu.VMEM((2, tm, KV_WIDTH), F32)],
        compiler_params=_params(("parallel", "parallel"), 40 << 20),
        name="nsa_inproj",
    )(x, g, wn, wt)


def _compress_kernel(sub_ref, pos_ref, w1_ref, w2_ref, w2t_ref, cn_ref, ct_ref):
    sub = sub_ref[0, 0]
    a = _dot((sub + pos_ref[0, 0]).astype(BF16), w1_ref[0, 0])
    bb = _dot((sub + pos_ref[0, 1]).astype(BF16), w1_ref[0, 1])
    nrow = sub.shape[0]
    hid = jax.nn.gelu(a + pltpu.roll(bb, nrow - 1, 0), approximate=True).astype(BF16)
    n_aug = cn_ref.shape[-1] - HEAD_DIM
    n = lax.broadcasted_iota(jnp.int32, (nrow, n_aug), 0)
    col = lax.broadcasted_iota(jnp.int32, (nrow, n_aug), 1)
    end_hi = n // (KEY_TILE // CMP_STRIDE)
    end_lo = (n % (KEY_TILE // CMP_STRIDE)) * CMP_STRIDE + (CMP_BLOCK - 1)
    aug = jnp.where(col < ALIBI_TERMS, end_hi, jnp.where(col < 2 * ALIBI_TERMS, end_lo, 0))
    cn_ref[0, 0] = jnp.concatenate([_dot(hid, w2_ref[0]).astype(BF16), aug.astype(F32).astype(BF16)], axis=1)
    sub_row = lax.broadcasted_iota(jnp.int32, (V_ROWS - HEAD_DIM, nrow), 0)
    ones_row = jnp.where(sub_row == 0, 1.0, 0.0).astype(BF16)
    vt = lax.dot_general(w2t_ref[0], hid, NT_DIMS, preferred_element_type=F32).astype(BF16)
    ct_ref[0, 0] = jnp.concatenate([vt, ones_row], axis=0)


def _compress(sub, pos, w1, w2, w2t):
    nc4, b, nsub, half = sub.shape
    dh = half // CMP_STRIDE
    return pl.pallas_call(
        _compress_kernel,
        grid=(nc4, b),
        in_specs=[
            pl.BlockSpec((1, 1, nsub, half), lambda c, bi: (c, bi, 0, 0)),
            pl.BlockSpec((1, 2, 1, half), lambda c, bi: (c // KV_GROUPS, 0, 0, 0)),
            pl.BlockSpec((1, 2, half, CMP_HIDDEN), lambda c, bi: (c // KV_GROUPS, 0, 0, 0)),
            pl.BlockSpec((1, CMP_HIDDEN, dh), lambda c, bi: (c // KV_GROUPS, 0, 0)),
            pl.BlockSpec((1, dh, CMP_HIDDEN), lambda c, bi: (c // KV_GROUPS, 0, 0)),
        ],
        out_specs=(
            pl.BlockSpec((1, 1, nsub, K_COLS), lambda c, bi: (c, bi, 0, 0)),
            pl.BlockSpec((1, 1, V_ROWS, nsub), lambda c, bi: (c, bi, 0, 0)),
        ),
        out_shape=(
            jax.ShapeDtypeStruct((nc4, b, nsub, K_COLS), BF16),
            jax.ShapeDtypeStruct((nc4, b, V_ROWS, nsub), BF16),
        ),
        compiler_params=_params(("parallel", "parallel"), 24 << 20),
        name="nsa_compress",
    )(sub, pos, w1, w2, w2t)


def _heads_on_lanes(qt):
    return jnp.concatenate([qt[r * HEAD_DIM:(r + 1) * HEAD_DIM] for r in range(HEADS_PER_GROUP)], axis=1)


def _heads_on_rows(o, tq):
    return jnp.concatenate([o[:, r * tq:(r + 1) * tq] for r in range(HEADS_PER_GROUP)], axis=0)


def _cmp_kernel(q_ref, qaug_ref, kc_ref, vct_ref, ov_ref, o_ref, sel_ref, cnt_ref, *, tq):
    i = pl.program_id(2)
    lanes = HEADS_PER_GROUP * tq
    q = jnp.concatenate([_heads_on_lanes(q_ref[0]), qaug_ref[0]], axis=0)
    s = _dot(kc_ref[0, 0], q)
    nc = s.shape[0]
    n = lax.broadcasted_iota(jnp.int32, (nc, lanes), 0)
    t_lane = i * tq + (lax.broadcasted_iota(jnp.int32, (1, lanes), 1) & (tq - 1))
    last = lax.shift_right_arithmetic(t_lane - (CMP_BLOCK - 1), CMP_STRIDE.bit_length() - 1)
    logits = jnp.where(n <= last, s, NEG)
    m = jnp.max(logits, axis=0, keepdims=True)
    e = jnp.exp2(logits - m).astype(BF16)
    n_sel = ov_ref.shape[0]
    prod = _dot(jnp.concatenate([vct_ref[0, 0], ov_ref[...]], axis=0), e)
    inv = jnp.where(last >= 0, 1.0 / prod[HEAD_DIM:HEAD_DIM + 1], 0.0)
    o_ref[0] = _heads_on_rows(prod[:HEAD_DIM] * inv, tq)
    imp_heads = prod[V_ROWS:] * inv
    imp = imp_heads[:, 0:tq]
    for r in range(1, HEADS_PER_GROUP):
        imp = imp + imp_heads[:, r * tq:(r + 1) * tq]

    j = lax.broadcasted_iota(jnp.int32, (n_sel, tq), 0)
    t_blk = (i * tq + lax.broadcasted_iota(jnp.int32, (n_sel, tq), 1)) // SEL_BLOCK
    forced = (j == 0) | (j == t_blk) | (j == t_blk - 1)
    imp = jnp.where(forced, imp + FORCE_BONUS, imp)
    imp = jnp.where(j > t_blk, NEG, imp)
    chosen = jnp.zeros((n_sel, tq), jnp.bool_)
    for _ in range(SEL_TOPK):
        mx = jnp.max(imp, axis=0, keepdims=True)
        first = jnp.min(jnp.where(imp == mx, j, n_sel), axis=0, keepdims=True)
        hit = j == first
        chosen = chosen | hit
        imp = jnp.where(hit, -jnp.inf, imp)
    sel = (chosen & (j <= t_blk)).astype(F32)
    sel_ref[0, 0] = sel
    ones = jnp.ones((8, Q_TILE), BF16)
    for sub in range(tq // Q_TILE):
        part = sel[:, sub * Q_TILE:(sub + 1) * Q_TILE].astype(BF16)
        cnt_ref[0, 0, sub] = lax.dot_general(ones, part, NT_DIMS, preferred_element_type=F32).astype(jnp.int32)


def _cmp_attention(qt, qaug, cn, ct, ov):
    b, _, s = qt.shape
    tq = min(CMP_Q_TILE, s)
    nq = s // tq
    sub_tiles = tq // Q_TILE
    nc = cn.shape[2]
    n_sel = s // SEL_BLOCK
    lanes = HEADS_PER_GROUP * tq
    rows = HEADS_PER_GROUP * HEAD_DIM
    return pl.pallas_call(
        functools.partial(_cmp_kernel, tq=tq),
        grid=(b, KV_GROUPS, nq),
        in_specs=[
            pl.BlockSpec((1, rows, tq), lambda bi, g, i: (bi, g, i)),
            pl.BlockSpec((1, K_COLS - HEAD_DIM, lanes), lambda bi, g, i: (g, 0, 0)),
            pl.BlockSpec((1, 1, nc, K_COLS), lambda bi, g, i: (g, bi, 0, 0)),
            pl.BlockSpec((1, 1, V_ROWS, nc), lambda bi, g, i: (KV_GROUPS + g, bi, 0, 0)),
            _const_spec((n_sel, nc)),
        ],
        out_specs=(
            pl.BlockSpec((1, rows, tq), lambda bi, g, i: (bi, g, i)),
            pl.BlockSpec((1, 1, n_sel, tq), lambda bi, g, i: (bi, g, 0, i)),
            pl.BlockSpec((1, 1, sub_tiles, 8, n_sel), lambda bi, g, i: (bi, g, i, 0, 0)),
        ),
        out_shape=(
            jax.ShapeDtypeStruct((b, NSA_WIDTH, s), F32),
            jax.ShapeDtypeStruct((b, KV_GROUPS, n_sel, s), F32),
            jax.ShapeDtypeStruct((b, KV_GROUPS, s // Q_TILE, 8, n_sel), jnp.int32),
        ),
        compiler_params=_params(("parallel", "parallel", "parallel"), 24 << 20),
        name="nsa_cmp_topk",
    )(qt, qaug, cn, ct, ov)


def _softmax_tile(pieces, consts, vs):
    tmax = None
    for s, c in zip(pieces, consts):
        t = jnp.max(s, axis=0, keepdims=True) + c
        tmax = t if tmax is None else jnp.maximum(tmax, t)
    p = jnp.concatenate([jnp.exp2(s - (tmax - c)).astype(BF16) for s, c in zip(pieces, consts)], axis=0)
    v = vs[0] if len(vs) == 1 else jnp.concatenate(vs, axis=1)
    return tmax, _dot(v, p)


def _softmax_merge(parts):
    m_new = parts[0][0]
    for m, _ in parts[1:]:
        m_new = jnp.maximum(m_new, m)
    acc = None
    for m, o in parts:
        term = jnp.exp2(m - m_new) * o
        acc = term if acc is None else acc + term
    return m_new, acc


def _selwin_kernel(cnt_ref, q_ref, qaug_ref, ks_ref, kw_ref, vs_ref, vw_ref, sel_ref, oc_ref, g_ref, slope_ref,
                   o_ref, act_ref, *, tq, nq):
    g = pl.program_id(1)
    step = pl.program_id(2)
    n_sel = sel_ref.shape[2]
    per_step = q_ref.shape[2] // tq
    slots = act_ref.shape[0] // per_step
    lanes = HEADS_PER_GROUP * tq
    tk = KEY_TILE
    half = SEL_BLOCK
    nwin = WINDOW // tk
    scan_unroll = next(c for c in (4, 2, 1) if nq % c == 0)
    slope = slope_ref[0]
    row = lax.broadcasted_iota(jnp.int32, (tk, lanes), 0)
    qoff = lax.broadcasted_iota(jnp.int32, (tk, lanes), 1) & (tq - 1)
    rel = qoff - row
    zero_c = jnp.zeros((1, lanes), F32)

    def tile4(v):
        return jnp.concatenate([v] * HEADS_PER_GROUP, axis=1)

    tiles_q = []
    for h in range(per_step):
        i = step * per_step + h
        cols = slice(h * tq, (h + 1) * tq)

        cnt_base = ((pl.program_id(0) * KV_GROUPS + g) * nq + i) * n_sel
        list_base = h * slots

        def scan(c, n, i=i, cnt_base=cnt_base, list_base=list_base):
            for u in range(scan_unroll):
                jj = c * scan_unroll + u
                hit = (jj < i) & ((cnt_ref[cnt_base + 2 * jj] + cnt_ref[cnt_base + 2 * jj + 1]) > 0)
                act_ref[list_base + n] = jj
                n = n + hit.astype(jnp.int32)
            return n

        n_act = lax.fori_loop(0, nq // scan_unroll, scan, 0)
        for k in range(slots - nq):
            act_ref[list_base + n_act + k] = 0
        tiles_q.append(dict(i=i, cols=cols, n_act=n_act, list_base=list_base, t0=i * tq))

    for tl in tiles_q:
        tl["q"] = jnp.concatenate([_heads_on_lanes(q_ref[0, :, tl["cols"]]), qaug_ref[0]], axis=0)

    def sel_tiles(tl, first, nslots):
        tiles = []
        for u in range(nslots):
            pos = first + u
            jj = act_ref[tl["list_base"] + pos]
            k0 = pl.multiple_of(jj * tk, tk)
            tiles.append((pos, jj, k0, _dot(ks_ref[0, 0, pl.ds(k0, tk), :], tl["q"])))

        def finish():
            parts = []
            for a in range(0, nslots, TILES_PER_SOFTMAX):
                pieces, consts, vs = [], [], []
                for pos, jj, k0, s in tiles[a:a + TILES_PER_SOFTMAX]:
                    off = slope * (k0 - tl["t0"]).astype(F32) + jnp.where(pos < tl["n_act"], 0.0, NEG)
                    for hb in range(2):
                        sel_row = sel_ref[0, 0, pl.ds(2 * jj + hb, 1), :][:, tl["cols"]]
                        pieces.append(s[hb * half:(hb + 1) * half])
                        consts.append(off + tile4((sel_row - 1.0) * (-NEG)))
                    vs.append(vs_ref[0, :, pl.ds(k0, tk)])
                parts.append(_softmax_tile(pieces, consts, vs))
            return parts

        return finish

    for tl in tiles_q:
        i = tl["i"]
        tl["k_diag"] = pl.multiple_of(tl["t0"], tk)
        tl["s_diag"] = _dot(ks_ref[0, 0, pl.ds(tl["k_diag"], tk), :], tl["q"])
        tl["first_sel"] = sel_tiles(tl, 0, SEL_FIRST_SLOTS)
        tl["win_k0"] = [pl.multiple_of(jnp.maximum(i - nwin + u, 0) * tk, tk) for u in range(nwin + 1)]
        tl["win_logits"] = [_dot(kw_ref[0, 0, pl.ds(k0, tk), :], tl["q"]) for k0 in tl["win_k0"]]

    for tl in tiles_q:
        i = tl["i"]
        sel_a = tile4(sel_ref[0, 0, pl.ds(2 * i, 1), :][:, tl["cols"]])
        sel_b = tile4(sel_ref[0, 0, pl.ds(2 * i + 1, 1), :][:, tl["cols"]])
        keep = (rel >= 0) & (jnp.where(row < half, sel_a, sel_b) > 0.5)
        diag = _softmax_tile([jnp.where(keep, tl["s_diag"], NEG)], [zero_c],
                             [vs_ref[0, :, pl.ds(tl["k_diag"], tk)]])
        tl["carry_s"] = _softmax_merge([diag] + tl["first_sel"]())

        parts = []
        for a in range(0, nwin + 1, TILES_PER_SOFTMAX):
            pieces, consts, vs = [], [], []
            for u in range(a, min(a + TILES_PER_SOFTMAX, nwin + 1)):
                s = tl["win_logits"][u]
                if u == 0:
                    s = jnp.where(rel < 0, s, NEG)
                if u == nwin:
                    s = jnp.where(rel >= 0, s, NEG)
                pieces.append(s)
                consts.append(slope * float((u - nwin) * tk) + jnp.where(i - nwin + u >= 0, 0.0, NEG))
                vs.append(vw_ref[0, :, pl.ds(tl["win_k0"][u], tk)])
            parts.append(_softmax_tile(pieces, consts, vs))
        tl["acc_w"] = _softmax_merge(parts)[1]

    for tl in tiles_q:
        def sel_more(gi, carry, tl=tl):
            first = SEL_FIRST_SLOTS + gi * SEL_LOOP_SLOTS
            return _softmax_merge([carry] + sel_tiles(tl, first, SEL_LOOP_SLOTS)())

        n_more = (tl["n_act"] - SEL_FIRST_SLOTS + SEL_LOOP_SLOTS - 1) // SEL_LOOP_SLOTS
        acc_s = lax.fori_loop(0, n_more, sel_more, tl["carry_s"])[1]
        acc_w = tl["acc_w"]

        o_s = acc_s[:HEAD_DIM] / acc_s[HEAD_DIM:HEAD_DIM + 1]
        o_w = acc_w[:HEAD_DIM] / acc_w[HEAD_DIM:HEAD_DIM + 1]
        outs = []
        for r in range(HEADS_PER_GROUP):
            grow = (g * HEADS_PER_GROUP + r) * N_BRANCH
            g_c = g_ref[0, pl.ds(grow, 1), :][:, tl["cols"]]
            g_s = g_ref[0, pl.ds(grow + 1, 1), :][:, tl["cols"]]
            g_w = g_ref[0, pl.ds(grow + 2, 1), :][:, tl["cols"]]
            outs.append(g_c * oc_ref[0, r * HEAD_DIM:(r + 1) * HEAD_DIM, tl["cols"]]
                        + g_s * o_s[:, r * tq:(r + 1) * tq]
                        + g_w * o_w[:, r * tq:(r + 1) * tq])
        o_ref[0, :, tl["cols"]] = jnp.concatenate(outs, axis=0).astype(BF16)


def _selwin_attention(qt, qaug, kn, vt, selm, selcnt, ocmp, gt, slopes):
    b, _, s = qt.shape
    tq = min(Q_TILE, s)
    assert tq == 2 * SEL_BLOCK and KEY_TILE == tq and s % tq == 0 and PROJ_ROWS % KEY_TILE == 0
    nq = s // tq
    n_sel = s // SEL_BLOCK
    lanes = HEADS_PER_GROUP * tq
    rows = HEADS_PER_GROUP * HEAD_DIM
    spare_slots = max(SEL_FIRST_SLOTS, SEL_LOOP_SLOTS)
    per_step = next(c for c in (SEL_STEP_TILES, 1) if nq % c == 0)
    tw = per_step * tq
    grid_spec = pltpu.PrefetchScalarGridSpec(
        num_scalar_prefetch=1,
        grid=(b, KV_GROUPS, nq // per_step),
        in_specs=[
            pl.BlockSpec((1, rows, tw), lambda bi, g, i, c: (bi, g, i)),
            pl.BlockSpec((1, K_COLS - HEAD_DIM, lanes), lambda bi, g, i, c: (g, 0, 0)),
            pl.BlockSpec((1, 1, s, K_COLS), lambda bi, g, i, c: (g, bi, 0, 0)),
            pl.BlockSpec((1, 1, s, K_COLS), lambda bi, g, i, c: (KV_GROUPS + g, bi, 0, 0)),
            pl.BlockSpec((1, V_ROWS, s), lambda bi, g, i, c: (bi, g, 0)),
            pl.BlockSpec((1, V_ROWS, s), lambda bi, g, i, c: (bi, KV_GROUPS + g, 0)),
            pl.BlockSpec((1, 1, n_sel, tw), lambda bi, g, i, c: (bi, g, 0, i)),
            pl.BlockSpec((1, rows, tw), lambda bi, g, i, c: (bi, g, i)),
            pl.BlockSpec((1, GATE_ROWS, tw), lambda bi, g, i, c: (bi, 0, i)),
            pl.BlockSpec((1, 1, lanes), lambda bi, g, i, c: (g, 0, 0)),
        ],
        out_specs=pl.BlockSpec((1, rows, tw), lambda bi, g, i, c: (bi, g, i)),
        scratch_shapes=[pltpu.SMEM((per_step * (nq + spare_slots),), jnp.int32)],
    )
    return pl.pallas_call(
        functools.partial(_selwin_kernel, tq=tq, nq=nq),
        grid_spec=grid_spec,
        out_shape=jax.ShapeDtypeStruct((b, NSA_WIDTH, s), BF16),
        compiler_params=_params(("parallel", "parallel", "arbitrary"), 32 << 20),
        name="nsa_sel_win",
    )(selcnt[:, :, :, 0, :].reshape(-1), qt, qaug, kn, kn, vt, vt, selm, ocmp, gt, slopes)


def _outproj_kernel(ot_ref, u_ref, halo_ref, x_ref, woa_ref, wob_ref, wp_ref, ps_ref, out_ref, *, tm):
    i = pl.program_id(1)
    u = u_ref[0]
    halo = jnp.where(i > 0, halo_ref[0], 0.0)
    ext = jnp.concatenate([halo, u], axis=0)
    t1 = i * tm + lax.broadcasted_iota(jnp.int32, (tm, 1), 0) + 1
    ys = []
    for gi, w in enumerate(POOL_WINDOWS):
        e = ext[:, gi * POOL_GROUP_DIM:(gi + 1) * POOL_GROUP_DIM]
        acc = e
        width = 1
        while width < w:
            acc = acc + pltpu.roll(acc, width, 0)
            width *= 2
        div = jnp.minimum(t1, w).astype(F32)
        dlt = acc[POOL_HALO:] / div - e[POOL_HALO:]
        ys.append(_dot(dlt.astype(BF16), wp_ref[gi]))
    y = jnp.concatenate(ys, axis=1) * ps_ref[...]
    mixed = lax.dot_general(ot_ref[0], woa_ref[...], TN_DIMS, preferred_element_type=F32)
    out_ref[0] = x_ref[0] + (mixed + _dot(y.astype(BF16), wob_ref[...]))


def _outproj(ot, u, x, woa, wob, wp, ps):
    b, s, d = x.shape
    tm = min(PROJ_ROWS, s)
    hb = tm // POOL_HALO
    return pl.pallas_call(
        functools.partial(_outproj_kernel, tm=tm),
        grid=(b, s // tm),
        in_specs=[
            pl.BlockSpec((1, NSA_WIDTH, tm), lambda bi, i: (bi, 0, i)),
            pl.BlockSpec((1, tm, POOL_WIDTH), lambda bi, i: (bi, i, 0)),
            pl.BlockSpec((1, POOL_HALO, POOL_WIDTH), lambda bi, i: (bi, jnp.maximum(i * hb - 1, 0), 0)),
            pl.BlockSpec((1, tm, d), lambda bi, i: (bi, i, 0)),
            _const_spec(woa.shape),
            _const_spec(wob.shape),
            _const_spec(wp.shape),
            _const_spec(ps.shape),
        ],
        out_specs=pl.BlockSpec((1, tm, d), lambda bi, i: (bi, i, 0)),
        out_shape=jax.ShapeDtypeStruct((b, s, d), F32),
        compiler_params=_params(("parallel", "parallel"), 40 << 20),
        name="nsa_outproj_pool",
    )(ot, u, u, x, woa, wob, wp, ps)


def _glu_kernel(x_ref, g_ref, w_ref, b_ref, z_ref):
    h = _rms(x_ref[...], g_ref[...]).astype(BF16)
    p = _dot(h, w_ref[...]) + b_ref[...]
    d = z_ref.shape[-1]
    z_ref[...] = p[:, :d] * jax.nn.sigmoid(p[:, d:])


def _glu(x2, g, w, bias):
    n, d = x2.shape
    tm = min(PROJ_ROWS, n)
    return pl.pallas_call(
        _glu_kernel,
        grid=(n // tm,),
        in_specs=[
            pl.BlockSpec((tm, d), lambda i: (i, 0)),
            _const_spec((1, d)),
            _const_spec(w.shape),
            _const_spec(bias.shape),
        ],
        out_specs=pl.BlockSpec((tm, d), lambda i: (i, 0)),
        out_shape=jax.ShapeDtypeStruct((n, d), F32),
        compiler_params=_params(("parallel",), 40 << 20),
        name="conv_pw1_glu",
    )(x2, g, w, bias)


def _dwconv_kernel(z_ref, halo_ref, x_ref, wdw_ref, bdw_ref, lng_ref, lnb_ref, w2_ref, b2_ref, out_ref,
                   zbuf, cbuf, *, tm):
    i = pl.program_id(1)
    zbuf[0:CONV_HALO, :] = jnp.where(i > 0, halo_ref[0], 0.0)
    zbuf[CONV_HALO:, :] = z_ref[0]
    d = z_ref.shape[-1]
    first = CONV_HALO - (CONV_WIDTH - 1)

    def col_body(c, carry):
        c0 = pl.multiple_of(c * LANES, LANES)
        for r0 in range(0, tm, CONV_ROW_CHUNK):
            acc = None
            for res in range(SUBLANES):
                off = (first + res) % SUBLANES
                base = r0 + first + res - off
                nrow = CONV_ROW_CHUNK + (SUBLANES if off else 0)
                y = None
                for k in range(res, CONV_WIDTH, SUBLANES):
                    lo = base + k - res
                    term = zbuf[lo:lo + nrow, pl.ds(c0, LANES)] * wdw_ref[k:k + 1, pl.ds(c0, LANES)]
                    y = term if y is None else y + term
                part = y[off:off + CONV_ROW_CHUNK]
                acc = part if acc is None else acc + part
            cbuf[r0:r0 + CONV_ROW_CHUNK, pl.ds(c0, LANES)] = acc
        return carry

    lax.fori_loop(0, d // LANES, col_body, 0)
    y = cbuf[...] + bdw_ref[...]
    mu = jnp.mean(y, axis=-1, keepdims=True)
    yc = y - mu
    var = jnp.mean(yc * yc, axis=-1, keepdims=True)
    y = yc * lax.rsqrt(var + 1e-5) * lng_ref[...] + lnb_ref[...]
    y = jax.nn.silu(y).astype(BF16)
    out_ref[0] = x_ref[0] + (_dot(y, w2_ref[...]) + b2_ref[...])


def _dwconv(z, x, wdw, bdw, lng, lnb, w2, b2):
    b, s, d = x.shape
    tm = min(CONV_ROWS, s)
    hb = tm // CONV_HALO
    return pl.pallas_call(
        functools.partial(_dwconv_kernel, tm=tm),
        grid=(b, s // tm),
        in_specs=[
            pl.BlockSpec((1, tm, d), lambda bi, i: (bi, i, 0)),
            pl.BlockSpec((1, CONV_HALO, d), lambda bi, i: (bi, jnp.maximum(i * hb - 1, 0), 0)),
            pl.BlockSpec((1, tm, d), lambda bi, i: (bi, i, 0)),
            _const_spec(wdw.shape),
            _const_spec(bdw.shape),
            _const_spec(lng.shape),
            _const_spec(lnb.shape),
            _const_spec(w2.shape),
            _const_spec(b2.shape),
        ],
        out_specs=pl.BlockSpec((1, tm, d), lambda bi, i: (bi, i, 0)),
        out_shape=jax.ShapeDtypeStruct((b, s, d), F32),
        scratch_shapes=[pltpu.VMEM((CONV_HALO + tm, d), F32), pltpu.VMEM((tm, d), F32)],
        compiler_params=_params(("parallel", "parallel"), 32 << 20),
        name="conv_dw_ln_pw2",
    )(z, z, x, wdw, bdw, lng, lnb, w2, b2)


def _ffn_kernel(x_ref, g_ref, wg_ref, wu_ref, wd_ref, fg_ref, out_ref, *, final):
    x = x_ref[...]
    h = _rms(x, g_ref[...]).astype(BF16)
    acc = jnp.zeros(x.shape, F32)
    for c in range(wd_ref.shape[0]):
        cols = slice(c * FFN_CHUNK, (c + 1) * FFN_CHUNK)
        a = _dot(h, wg_ref[:, cols])
        up = _dot(h, wu_ref[:, cols])
        acc = acc + _dot((jax.nn.silu(a) * up).astype(BF16), wd_ref[c])
    y = x + acc
    if final:
        y = _rms(y, fg_ref[...])
    out_ref[...] = y


def _ffn(x2, g, wg, wu, wd, fg, final):
    n, d = x2.shape
    tm = min(FFN_ROWS, n)
    return pl.pallas_call(
        functools.partial(_ffn_kernel, final=final),
        grid=(n // tm,),
        in_specs=[
            pl.BlockSpec((tm, d), lambda i: (i, 0)),
            _const_spec((1, d)),
            _const_spec(wg.shape),
            _const_spec(wu.shape),
            _const_spec(wd.shape),
            _const_spec((1, d)),
        ],
        out_specs=pl.BlockSpec((tm, d), lambda i: (i, 0)),
        out_shape=jax.ShapeDtypeStruct((n, d), F32),
        compiler_params=_params(("parallel",), 52 << 20),
        name="ffn_swiglu",
    )(x2, g, wg, wu, wd, fg)


def _overlap_matrix(n_sel, n_cmp_rows):
    c0 = np.arange(n_cmp_rows)[None, :] * CMP_STRIDE
    s0 = np.arange(n_sel)[:, None] * SEL_BLOCK
    ov = np.clip(np.minimum(c0 + CMP_BLOCK, s0 + SEL_BLOCK) - np.maximum(c0, s0), 0, None) / CMP_BLOCK
    return jnp.asarray(ov, BF16)


def _alibi_lane_slopes(tq):
    h = np.arange(1, NSA_HEADS + 1, dtype=np.float64)
    sl = np.exp2(-8.0 * h / NSA_HEADS).reshape(KV_GROUPS, HEADS_PER_GROUP) * LOG2E
    lane = jnp.asarray(np.repeat(sl, tq, axis=1)[:, None, :], F32)
    terms, rem = [], lane
    for _ in range(ALIBI_TERMS):
        t = rem.astype(BF16)
        terms.append(t)
        rem = rem - t.astype(F32)
    pad = jnp.zeros((KV_GROUPS, K_COLS - HEAD_DIM - ALIBI_TERMS, lane.shape[-1]), BF16)
    hi_lo = [t * float(KEY_TILE) for t in terms] + terms
    return lane, jnp.concatenate(terms + [pad], axis=1), jnp.concatenate(hi_lo + [pad[:, ALIBI_TERMS:]], axis=1)


def _split_w_in(w_in):
    d = w_in.shape[0]
    o = 0
    q = w_in[:, o:o + NSA_WIDTH]; o += NSA_WIDTH
    kc = w_in[:, o:o + KV_WIDTH]; o += KV_WIDTH
    vc = w_in[:, o:o + KV_WIDTH]; o += KV_WIDTH
    ks = w_in[:, o:o + KV_WIDTH]; o += KV_WIDTH
    vs = w_in[:, o:o + KV_WIDTH]; o += KV_WIDTH
    kw = w_in[:, o:o + KV_WIDTH]; o += KV_WIDTH
    vw = w_in[:, o:o + KV_WIDTH]; o += KV_WIDTH
    gate = w_in[:, o:o + GATE_WIDTH]; o += GATE_WIDTH
    u = w_in[:, o:]
    wn = jnp.concatenate([kc, vc, ks, kw, u], axis=1).astype(BF16)
    pad = jnp.zeros((d, GATE_ROWS - GATE_WIDTH), w_in.dtype)
    wt = jnp.concatenate([q, vs, vw, gate, pad], axis=1).T.astype(BF16)
    return wn, wt


def _even_mixer(x, g, w_in, pos_k, pos_v, k_w1, k_w2, v_w1, v_w2, w_pool, pool_scale, w_out):
    b, s, d = x.shape
    dh = HEAD_DIM
    half = CMP_STRIDE * dh
    wn, wt = _split_w_in(w_in)
    cmp_in, kn, u, qt, vt, gt = _inproj(x, g.reshape(1, d), wn, wt)

    pos = jnp.stack([pos_k.reshape(2, 1, half), pos_v.reshape(2, 1, half)])
    w1 = jnp.stack([k_w1.reshape(2, half, CMP_HIDDEN), v_w1.reshape(2, half, CMP_HIDDEN)]).astype(BF16)
    w2 = jnp.stack([k_w2, v_w2]).astype(BF16)
    w2t = jnp.stack([k_w2.T, v_w2.T]).astype(BF16)
    cn, ct = _compress(cmp_in, pos, w1, w2, w2t)

    tq = min(Q_TILE, s)
    slopes, qaug, _ = _alibi_lane_slopes(tq)
    ov = _overlap_matrix(s // SEL_BLOCK, s // CMP_STRIDE)
    ocmp, selm, selcnt = _cmp_attention(qt, _alibi_lane_slopes(min(CMP_Q_TILE, s))[2], cn, ct, ov)
    ot = _selwin_attention(qt, qaug, kn, vt, selm, selcnt, ocmp, gt, slopes)

    woa = w_out[:NSA_WIDTH].astype(BF16)
    wob = w_out[NSA_WIDTH:].astype(BF16)
    return _outproj(ot, u, x, woa, wob, w_pool.astype(BF16), pool_scale.reshape(1, -1))


def _odd_mixer(x, g, w_pw1, b_pw1, w_dw, b_dw, ln_g, ln_b, w_pw2, b_pw2):
    b, s, d = x.shape
    z = _glu(x.reshape(b * s, d), g.reshape(1, d), w_pw1.astype(BF16), b_pw1.reshape(1, -1))
    wdw = jnp.concatenate([w_dw, jnp.zeros((CONV_HALO - CONV_WIDTH, d), w_dw.dtype)], axis=0)
    return _dwconv(z.reshape(b, s, d), x, wdw, b_dw.reshape(1, d), ln_g.reshape(1, d), ln_b.reshape(1, d),
                   w_pw2.astype(BF16), b_pw2.reshape(1, d))


def _ffn_layer(x, g, w_gate, w_up, w_down, final_g, final):
    b, s, d = x.shape
    hid = w_gate.shape[1]
    nch = hid // FFN_CHUNK
    wg = w_gate.astype(BF16)
    wu = w_up.astype(BF16)
    wd = w_down.reshape(nch, FFN_CHUNK, d).astype(BF16)
    y = _ffn(x.reshape(b * s, d), g.reshape(1, d), wg, wu, wd, final_g.reshape(1, d), final)
    return y.reshape(b, s, d)


def kernel(x, mix_norm, ffn_norm, nsa_w_in, cmp_pos_k, cmp_pos_v, cmp_k_w1, cmp_k_w2, cmp_v_w1, cmp_v_w2,
           pool_w, pool_scale, mix_w_out, conv_w_pw1, conv_b_pw1, conv_w_dw, conv_b_dw, conv_ln_g, conv_ln_b,
           conv_w_pw2, conv_b_pw2, ffn_w_gate, ffn_w_up, ffn_w_down, final_norm):
    depth = mix_norm.shape[0]
    for layer in range(depth):
        i = layer // 2
        if layer % 2 == 0:
            x = _even_mixer(x, mix_norm[layer], nsa_w_in[i], cmp_pos_k[i], cmp_pos_v[i], cmp_k_w1[i],
                            cmp_k_w2[i], cmp_v_w1[i], cmp_v_w2[i], pool_w[i], pool_scale[i], mix_w_out[i])
        else:
            x = _odd_mixer(x, mix_norm[layer], conv_w_pw1[i], conv_b_pw1[i], conv_w_dw[i], conv_b_dw[i],
                           conv_ln_g[i], conv_ln_b[i], conv_w_pw2[i], conv_b_pw2[i])
        x = _ffn_layer(x, ffn_norm[layer], ffn_w_gate[layer], ffn_w_up[layer], ffn_w_down[layer],
                       final_norm, final=(layer == depth - 1))
    return x
```
